```python
import math
import jax, jax.numpy as jnp
from jax import lax
import numpy as np

D_MODEL = 2048
BATCH = 2
SEQ = 16384
DEPTH = 2

LRU_WIDTH = 2048
LRU_BLOCKS = 16
LRU_BLOCK = LRU_WIDTH // LRU_BLOCKS
CONV_WIDTH = 4
LRU_C = 8.0
HEAD_DIM = 128
HEADS_PER_GROUP = 4
DILATED_GROUPS = ((128, 1), (512, 4), (2048, 16))
N_ATTN_HEADS = HEADS_PER_GROUP * len(DILATED_GROUPS)
ATTN_WIDTH = N_ATTN_HEADS * HEAD_DIM
ATTN_OUT_WIDTH = HEADS_PER_GROUP * HEAD_DIM
Q_BLOCK = 128
ROPE_THETA = 500000.0
ROT_DIM = HEAD_DIM // 4
N_BRANCHES = 2
IN_SPLITS = [LRU_WIDTH, 2 * LRU_WIDTH, 2 * LRU_WIDTH + ATTN_WIDTH,
             2 * LRU_WIDTH + 2 * ATTN_WIDTH, 2 * LRU_WIDTH + 3 * ATTN_WIDTH,
             2 * LRU_WIDTH + 3 * ATTN_WIDTH + D_MODEL]
IN_WIDTH = 2 * LRU_WIDTH + 3 * ATTN_WIDTH + N_BRANCHES * D_MODEL
N_EXPERTS = 64
TOP_K = 8
N_EXPERT_GROUPS = 8
TOPK_GROUPS = 4
EXPERT_HIDDEN = 512
SHARED_HIDDEN = 512
ROUTED_SCALE = 2.5
DISPATCH_BLOCK = 256
LN_EPS = 1e-5
ALPHA = (2 * DEPTH) ** 0.25
BETA = (8 * DEPTH) ** -0.25

kernel_name = "hawk_longnet_moe_hybrid"


def layer_norm(x, g, b):
    xf = x.astype(jnp.float32)
    mu = jnp.mean(xf, axis=-1, keepdims=True)
    var = jnp.mean(jnp.square(xf - mu), axis=-1, keepdims=True)
    y = (xf - mu) * lax.rsqrt(var + LN_EPS) * g.astype(jnp.float32) + b.astype(jnp.float32)
    return y.astype(x.dtype)


def partial_rope(t, pos):
    half = ROT_DIM // 2
    inv_freq = jnp.power(ROPE_THETA, -jnp.arange(half, dtype=jnp.float32) * (2.0 / ROT_DIM))
    ang = pos.astype(jnp.float32)[:, None] * inv_freq[None, :]
    cos = jnp.cos(ang)[None, :, None, :]
    sin = jnp.sin(ang)[None, :, None, :]
    tf = t.astype(jnp.float32)
    x1, x2 = tf[..., :half], tf[..., half:ROT_DIM]
    out = jnp.concatenate([x1 * cos - x2 * sin, x2 * cos + x1 * sin, tf[..., ROT_DIM:]], axis=-1)
    return out.astype(t.dtype)


def dilated_window_attention(q, k, v, window, dilation):
    B, S, H, hd = q.shape
    d = dilation
    band = window // dilation
    L = S // d
    nb = -(-L // Q_BLOCK)
    Lp = nb * Q_BLOCK

    def to_sub(t):
        t = t.reshape(B, L, d, H, hd).transpose(0, 3, 2, 1, 4)
        t = jnp.pad(t, ((0, 0), (0, 0), (0, 0), (0, Lp - L), (0, 0)))
        return t.reshape(B, H, d, nb, Q_BLOCK, hd)

    qs, ks, vs = to_sub(q), to_sub(k), to_sub(v)
    shift = ((0, 0), (0, 0), (0, 0), (1, 0), (0, 0), (0, 0))
    k_cat = jnp.concatenate([jnp.pad(ks[:, :, :, :-1], shift), ks], axis=4)
    v_cat = jnp.concatenate([jnp.pad(vs[:, :, :, :-1], shift), vs], axis=4)
    scores = jnp.einsum('bhrnqe,bhrnke->bhrnqk', qs, k_cat).astype(jnp.float32) * (hd ** -0.5)
    qi = jnp.arange(Q_BLOCK)[:, None] + Q_BLOCK
    ki = jnp.arange(2 * Q_BLOCK)[None, :]
    delta = qi - ki
    local = (delta >= 0) & (delta <= band)
    no_prev = (jnp.arange(nb)[:, None, None] == 0) & (ki[None] < Q_BLOCK)
    mask = local[None] & ~no_prev
    scores = jnp.where(mask, scores, -jnp.inf)
    lse = jax.nn.logsumexp(scores, axis=-1)
    p = jnp.exp(scores - lse[..., None])
    o = jnp.einsum('bhrnqk,bhrnke->bhrnqe', p.astype(v.dtype), v_cat)
    o = o.reshape(B, H, d, Lp, hd)[:, :, :, :L].transpose(0, 3, 2, 1, 4).reshape(B, S, H, hd)
    lse = lse.reshape(B, H, d, Lp)[:, :, :, :L].transpose(0, 3, 2, 1).reshape(B, S, H)
    return o, lse


def rg_lru_branch(u, y_pre, conv_w, conv_b, w_a, b_a, w_x, b_x, lam):
    B, S, C = u.shape
    uc = lax.conv_general_dilated(u, conv_w[:, None, :], window_strides=(1,),
                                  padding=[(CONV_WIDTH - 1, 0)],
                                  dimension_numbers=('NWC', 'WIO', 'NWC'),
                                  feature_group_count=C) + conv_b
    uh = uc.reshape(B, S, LRU_BLOCKS, LRU_BLOCK)
    r = jax.nn.sigmoid(jnp.einsum('bshi,hij->bshj', uh, w_a).reshape(B, S, C) + b_a)
    i = jax.nn.sigmoid(jnp.einsum('bshi,hij->bshj', uh, w_x).reshape(B, S, C) + b_x)
    log_a = -LRU_C * jax.nn.softplus(-lam.astype(jnp.float32)) * r.astype(jnp.float32)
    a = jnp.exp(log_a)
    b = jnp.sqrt(-jnp.expm1(2.0 * log_a)) * (i * uc).astype(jnp.float32)

    def combine(left, right):
        a_l, b_l = left
        a_r, b_r = right
        return a_l * a_r, a_r * b_l + b_r

    _, h = lax.associative_scan(combine, (a, b), axis=1)
    return h.astype(u.dtype) * jax.nn.gelu(y_pre)


def hybrid_mixer(x, w_in, conv_w, conv_b, lru_wa, lru_ba, lru_wx, lru_bx, lru_lambda,
                 w_lru_proj, w_attn_proj, w_out):
    B, S, _ = x.shape
    proj = jnp.einsum('bsd,de->bse', x, w_in)
    u, y_pre, q, k, v, g_lru, g_attn = jnp.split(proj, IN_SPLITS, axis=-1)
    lru = rg_lru_branch(u, y_pre, conv_w, conv_b, lru_wa, lru_ba, lru_wx, lru_bx, lru_lambda)

    pos = jnp.arange(S, dtype=jnp.int32)
    q = partial_rope(q.reshape(B, S, N_ATTN_HEADS, HEAD_DIM), pos)
    k = partial_rope(k.reshape(B, S, N_ATTN_HEADS, HEAD_DIM), pos)
    v = v.reshape(B, S, N_ATTN_HEADS, HEAD_DIM)
    outs, lses = [], []
    for g, (window, dilation) in enumerate(DILATED_GROUPS):
        hs = slice(g * HEADS_PER_GROUP, (g + 1) * HEADS_PER_GROUP)
        o, l = dilated_window_attention(q[:, :, hs], k[:, :, hs], v[:, :, hs], window, dilation)
        outs.append(o)
        lses.append(l)
    wts = jax.nn.softmax(jnp.stack(lses, axis=0), axis=0)
    attn = jnp.sum(wts[..., None] * jnp.stack(outs, axis=0).astype(jnp.float32), axis=0)
    attn = attn.astype(x.dtype).reshape(B, S, ATTN_OUT_WIDTH)

    merged = (jax.nn.sigmoid(g_lru) * jnp.einsum('bsc,cd->bsd', lru, w_lru_proj)
              + jax.nn.sigmoid(g_attn) * jnp.einsum('bsc,cd->bsd', attn, w_attn_proj))
    return jnp.einsum('bsd,de->bse', merged, w_out)


def moe_ffn(x, router_w, router_bias, w_gate, w_up, w_down, sh_gate, sh_up, sh_down):
    B, S, D = x.shape
    T = B * S
    x2 = x.reshape(T, D)
    scores = jax.nn.sigmoid(x2.astype(jnp.float32) @ router_w.astype(jnp.float32))
    biased = scores + router_bias.astype(jnp.float32)
    grouped = biased.reshape(T, N_EXPERT_GROUPS, N_EXPERTS // N_EXPERT_GROUPS)
    group_score = jnp.sum(lax.top_k(grouped, 2)[0], axis=-1)
    _, gidx = lax.top_k(group_score, TOPK_GROUPS)
    group_mask = jnp.sum(jax.nn.one_hot(gidx, N_EXPERT_GROUPS, dtype=jnp.float32), axis=1) > 0
    expert_mask = jnp.repeat(group_mask, N_EXPERTS // N_EXPERT_GROUPS, axis=1)
    _, eidx = lax.top_k(jnp.where(expert_mask, biased, -jnp.inf), TOP_K)
    gate = jnp.take_along_axis(scores, eidx, axis=1)
    gate = gate / jnp.sum(gate, axis=-1, keepdims=True) * ROUTED_SCALE

    n_assign = T * TOP_K
    n_rows = -(-n_assign // DISPATCH_BLOCK) * DISPATCH_BLOCK + N_EXPERTS * DISPATCH_BLOCK
    n_blocks = n_rows // DISPATCH_BLOCK
    flat_e = eidx.reshape(-1)
    flat_t = jnp.arange(n_assign, dtype=jnp.int32) // TOP_K
    flat_w = gate.reshape(-1)
    order = jnp.argsort(flat_e)
    sorted_e = flat_e[order]
    counts = jnp.bincount(flat_e, length=N_EXPERTS)
    starts = jnp.cumsum(counts) - counts
    padded = (counts + DISPATCH_BLOCK - 1) // DISPATCH_BLOCK * DISPATCH_BLOCK
    padded_end = jnp.cumsum(padded)
    padded_start = padded_end - padded
    dest = padded_start[sorted_e] + jnp.arange(n_assign) - starts[sorted_e]
    row_tok = jnp.full((n_rows,), T, dtype=jnp.int32).at[dest].set(flat_t[order])
    row_w = jnp.zeros((n_rows,), jnp.float32).at[dest].set(flat_w[order])
    block_e = jnp.minimum(jnp.searchsorted(padded_end, jnp.arange(n_blocks) * DISPATCH_BLOCK,
                                           side='right'), N_EXPERTS - 1)
    x_pad = jnp.concatenate([x2, jnp.zeros((1, D), x2.dtype)], axis=0)

    def expert_block(acc, blk):
        tok, wt, e = blk
        xb = x_pad[tok]
        h = jax.nn.silu(xb @ w_gate[e]) * (xb @ w_up[e])
        yb = (h @ w_down[e]) * wt[:, None].astype(x2.dtype)
        return acc.at[tok].add(yb), None

    acc, _ = lax.scan(expert_block, jnp.zeros((T + 1, D), x2.dtype),
                      (row_tok.reshape(n_blocks, DISPATCH_BLOCK),
                       row_w.reshape(n_blocks, DISPATCH_BLOCK), block_e))
    shared = (jax.nn.silu(x2 @ sh_gate) * (x2 @ sh_up)) @ sh_down
    return (acc[:T] + shared).reshape(B, S, D)


def setup_inputs(seed: int = 0) -> dict:
    key = jax.random.key(seed)
    ks = jax.random.split(key, 24)

    def nrm(k, shape, scale):
        return jax.random.normal(k, shape, jnp.float32) * scale

    v_scale = jnp.concatenate([jnp.ones((2 * LRU_WIDTH + 2 * ATTN_WIDTH,), jnp.float32),
                               jnp.full((ATTN_WIDTH,), BETA, jnp.float32),
                               jnp.ones((N_BRANCHES * D_MODEL,), jnp.float32)])
    a_c = jax.random.uniform(ks[8], (DEPTH, LRU_WIDTH), jnp.float32, 0.9, 0.999)
    a0 = a_c ** (1.0 / LRU_C)
    return {
        "x": nrm(ks[0], (BATCH, SEQ, D_MODEL), 1.0),
        "w_in": nrm(ks[1], (DEPTH, D_MODEL, IN_WIDTH), D_MODEL ** -0.5) * v_scale,
        "conv_w": nrm(ks[2], (DEPTH, CONV_WIDTH, LRU_WIDTH), CONV_WIDTH ** -0.5),
        "conv_b": nrm(ks[3], (DEPTH, LRU_WIDTH), 0.02),
        "lru_wa": nrm(ks[4], (DEPTH, LRU_BLOCKS, LRU_BLOCK, LRU_BLOCK), LRU_BLOCK ** -0.5),
        "lru_ba": nrm(ks[5], (DEPTH, LRU_WIDTH), 0.02),
        "lru_wx": nrm(ks[6], (DEPTH, LRU_BLOCKS, LRU_BLOCK, LRU_BLOCK), LRU_BLOCK ** -0.5),
        "lru_bx": nrm(ks[7], (DEPTH, LRU_WIDTH), 0.02),
        "lru_lambda": jnp.log(a0) - jnp.log1p(-a0),
        "w_lru_proj": nrm(ks[9], (DEPTH, LRU_WIDTH, D_MODEL), BETA * LRU_WIDTH ** -0.5),
        "w_attn_proj": nrm(ks[10], (DEPTH, ATTN_OUT_WIDTH, D_MODEL), BETA * ATTN_OUT_WIDTH ** -0.5),
        "w_out": nrm(ks[11], (DEPTH, D_MODEL, D_MODEL), BETA * D_MODEL ** -0.5),
        "ln1_g": 1.0 + nrm(ks[12], (DEPTH, D_MODEL), 0.02),
        "ln1_b": nrm(ks[13], (DEPTH, D_MODEL), 0.02),
        "router_w": nrm(ks[14], (DEPTH, D_MODEL, N_EXPERTS), D_MODEL ** -0.5),
        "router_bias": nrm(ks[15], (DEPTH, N_EXPERTS), 0.01),
        "exp_w_gate": nrm(ks[16], (DEPTH, N_EXPERTS, D_MODEL, EXPERT_HIDDEN), D_MODEL ** -0.5),
        "exp_w_up": nrm(ks[17], (DEPTH, N_EXPERTS, D_MODEL, EXPERT_HIDDEN), D_MODEL ** -0.5),
        "exp_w_down": nrm(ks[18], (DEPTH, N_EXPERTS, EXPERT_HIDDEN, D_MODEL), BETA * EXPERT_HIDDEN ** -0.5),
        "sh_w_gate": nrm(ks[19], (DEPTH, D_MODEL, SHARED_HIDDEN), D_MODEL ** -0.5),
        "sh_w_up": nrm(ks[20], (DEPTH, D_MODEL, SHARED_HIDDEN), D_MODEL ** -0.5),
        "sh_w_down": nrm(ks[21], (DEPTH, SHARED_HIDDEN, D_MODEL), BETA * SHARED_HIDDEN ** -0.5),
        "ln2_g": 1.0 + nrm(ks[22], (DEPTH, D_MODEL), 0.02),
        "ln2_b": nrm(ks[23], (DEPTH, D_MODEL), 0.02),
    }


def reference(x, w_in, conv_w, conv_b, lru_wa, lru_ba, lru_wx, lru_bx, lru_lambda,
              w_lru_proj, w_attn_proj, w_out, ln1_g, ln1_b, router_w, router_bias,
              exp_w_gate, exp_w_up, exp_w_down, sh_w_gate, sh_w_up, sh_w_down, ln2_g, ln2_b):
    for l in range(DEPTH):
        mix = hybrid_mixer(x, w_in[l], conv_w[l], conv_b[l], lru_wa[l], lru_ba[l], lru_wx[l],
                           lru_bx[l], lru_lambda[l], w_lru_proj[l], w_attn_proj[l], w_out[l])
        x = layer_norm(ALPHA * x + mix, ln1_g[l], ln1_b[l])
        ffn = moe_ffn(x, router_w[l], router_bias[l], exp_w_gate[l], exp_w_up[l], exp_w_down[l],
                      sh_w_gate[l], sh_w_up[l], sh_w_down[l])
        x = layer_norm(ALPHA * x + ffn, ln2_g[l], ln2_b[l])
    return x
```

```python
import functools
import math

import jax
import jax.numpy as jnp
from jax import lax
from jax.experimental import pallas as pl
from jax.experimental.pallas import tpu as pltpu

F32 = jnp.float32
BF16 = jnp.bfloat16
U32 = jnp.uint32
I32 = jnp.int32

LANES = 128
SUBLANES = 8

DEPTH = 2
LRU_BLOCK = 128
CONV_WIDTH = 4
LRU_C = 8.0
HEAD_DIM = 128
HEADS_PER_GROUP = 4
DILATED_GROUPS = ((128, 1), (512, 4), (2048, 16))
Q_BLOCK = 128
ROPE_THETA = 500000.0
ROT_DIM = HEAD_DIM // 4
N_EXPERTS = 64
TOP_K = 8
N_EXPERT_GROUPS = 8
GROUP_SIZE = N_EXPERTS // N_EXPERT_GROUPS
TOPK_GROUPS = 4
ROUTED_SCALE = 2.5
LN_EPS = 1e-5
ALPHA = (2 * DEPTH) ** 0.25

GROUP_WIDTH = HEADS_PER_GROUP * HEAD_DIM
HI_MASK = 0xFFFF0000


def _params(sem, vmem_mb):
    return pltpu.CompilerParams(dimension_semantics=sem, vmem_limit_bytes=vmem_mb * 2 ** 20)


def _sigmoid(x):
    return 1.0 / (1.0 + jnp.exp(-x))


def _silu(x):
    return x * _sigmoid(x)


def _layer_norm(y, g, b):
    mu = jnp.mean(y, axis=-1, keepdims=True)
    yc = y - mu
    var = jnp.mean(yc * yc, axis=-1, keepdims=True)
    return yc * lax.rsqrt(var + LN_EPS) * g + b


def _pack_halves(y):
    c = y.shape[1] // 2
    lo = lax.bitcast_convert_type(y[:, :c].astype(BF16).astype(F32), U32)
    hi = lax.bitcast_convert_type(y[:, c:].astype(BF16).astype(F32), U32)
    return (lo >> 16) | (hi & jnp.uint32(HI_MASK))


def _unpack_halves(w):
    lo = lax.bitcast_convert_type(w << 16, F32)
    hi = lax.bitcast_convert_type(w & jnp.uint32(HI_MASK), F32)
    return lo, hi


def _matmul_kernel(a_ref, b_ref, o_ref):
    o_ref[...] = jnp.dot(a_ref[...], b_ref[...], preferred_element_type=F32).astype(o_ref.dtype)


def _matmul(a, b, bm, bn, out_dtype):
    m, k = a.shape
    n = b.shape[1]
    bm = min(bm, m)
    return pl.pallas_call(
        _matmul_kernel,
        grid=(n // bn, m // bm),
        in_specs=[pl.BlockSpec((bm, k), lambda j, i: (i, 0)),
                  pl.BlockSpec((k, bn), lambda j, i: (0, j))],
        out_specs=pl.BlockSpec((bm, bn), lambda j, i: (i, j)),
        out_shape=jax.ShapeDtypeStruct((m, n), out_dtype),
        compiler_params=_params(("parallel", "arbitrary"), 48),
        name="in_proj",
    )(a, b)


def _lru_kernel(u_ref, y_ref, cw_ref, cb_ref, wa_ref, ba_ref, wx_ref, bx_ref, lam_ref,
                o_ref, tail_ref, h_ref, *, tc, nc):
    @pl.when(pl.program_id(2) == 0)
    def _():
        tail_ref[...] = jnp.zeros_like(tail_ref)
        h_ref[...] = jnp.zeros_like(h_ref)

    row8 = lax.broadcasted_iota(I32, (SUBLANES, LANES), 0)
    ones8 = jnp.ones((SUBLANES, LANES), F32)
    zeros8 = jnp.zeros((SUBLANES, LANES), F32)

    def shift_small(x, head8, j):
        r = pltpu.roll(x, j, 0)
        first = jnp.where(row8 < j, pltpu.roll(head8, j, 0), r[:SUBLANES])
        return jnp.concatenate([first, r[SUBLANES:]], axis=0)

    def shift_big(x, fill, d):
        return jnp.concatenate([jnp.full((d, LANES), fill, F32), x[:tc - d]], axis=0)

    for c in range(nc):
        cs = slice(c * LANES, (c + 1) * LANES)
        u = u_ref[:, cs].astype(F32)
        tail = tail_ref[:, cs]
        cw = cw_ref[:, cs]
        uc = u * cw[CONV_WIDTH - 1:CONV_WIDTH, :] + cb_ref[:, cs]
        for j in range(1, CONV_WIDTH):
            uc = uc + shift_small(u, tail, j) * cw[CONV_WIDTH - 1 - j:CONV_WIDTH - j, :]
        tail_ref[:, cs] = u[tc - SUBLANES:, :]

        ub = uc.astype(BF16)
        r = _sigmoid(jnp.dot(ub, wa_ref[c], preferred_element_type=F32) + ba_ref[:, cs])
        i = _sigmoid(jnp.dot(ub, wx_ref[c], preferred_element_type=F32) + bx_ref[:, cs])
        z = -lam_ref[:, cs]
        softplus = jnp.maximum(z, 0.0) + jnp.log(1.0 + jnp.exp(-jnp.abs(z)))
        a = jnp.exp((-LRU_C * softplus) * r)
        b = jnp.sqrt(1.0 - a * a) * (i * uc)

        d = 1
        while d < tc:
            if d < SUBLANES:
                a_sh = shift_small(a, ones8, d)
                b_sh = shift_small(b, zeros8, d)
            else:
                a_sh = shift_big(a, 1.0, d)
                b_sh = shift_big(b, 0.0, d)
            b = a * b_sh + b
            a = a * a_sh
            d *= 2
        h = b + a * h_ref[:, cs]
        h_ref[:, cs] = h[tc - 1:tc, :]

        y = y_ref[:, cs].astype(F32)
        gelu = 0.5 * y * (1.0 + jnp.tanh(math.sqrt(2.0 / math.pi) * (y + 0.044715 * (y * y * y))))
        o_ref[:, cs] = (h * gelu).astype(o_ref.dtype)


def _lru(proj, conv_w, conv_b, wa, ba, wx, bx, lam, batch, seq, width, col_u, col_y):
    tc, nc = 256, 4
    tc = min(tc, seq)
    cw = nc * LANES
    nt = seq // tc
    t_total = batch * seq
    vec = pl.BlockSpec((1, cw), lambda b, c, t: (0, c))
    gate = pl.BlockSpec((nc, LRU_BLOCK, LRU_BLOCK), lambda b, c, t: (c, 0, 0))
    return pl.pallas_call(
        functools.partial(_lru_kernel, tc=tc, nc=nc),
        grid=(batch, width // cw, nt),
        in_specs=[pl.BlockSpec((tc, cw), lambda b, c, t: (b * nt + t, col_u // cw + c)),
                  pl.BlockSpec((tc, cw), lambda b, c, t: (b * nt + t, col_y // cw + c)),
                  pl.BlockSpec((CONV_WIDTH, cw), lambda b, c, t: (0, c)),
                  vec, gate, vec, gate, vec, vec],
        out_specs=pl.BlockSpec((tc, cw), lambda b, c, t: (b * nt + t, c)),
        out_shape=jax.ShapeDtypeStruct((t_total, width), BF16),
        scratch_shapes=[pltpu.VMEM((SUBLANES, cw), F32), pltpu.VMEM((1, cw), F32)],
        compiler_params=_params(("parallel", "parallel", "arbitrary"), 32),
        name="lru",
    )(proj, proj, conv_w, conv_b, wa, ba, wx, bx, lam)


def _attn_kernel(q_ref, kc_ref, kp_ref, vc_ref, vp_ref, cc_ref, sc_ref, cp_ref, sp_ref,
                 o_ref, lse_ref):
    n = pl.program_id(2)
    lane = lax.broadcasted_iota(I32, (Q_BLOCK, HEAD_DIM), 1)
    half = ROT_DIM // 2

    def rope(t, cos, sin):
        swapped = jnp.where(lane < half, pltpu.roll(t, HEAD_DIM - half, 1), pltpu.roll(t, half, 1))
        return t * cos + swapped * sin

    cos_c, sin_c = cc_ref[...], sc_ref[...]
    cos_p, sin_p = cp_ref[...], sp_ref[...]
    qi = lax.broadcasted_iota(I32, (Q_BLOCK, 2 * Q_BLOCK), 0)
    ki = lax.broadcasted_iota(I32, (Q_BLOCK, 2 * Q_BLOCK), 1)
    delta = qi + Q_BLOCK - ki
    mask = (delta >= 0) & (delta <= Q_BLOCK) & jnp.logical_not((n == 0) & (ki < Q_BLOCK))
    lse_lane = lax.broadcasted_iota(I32, (Q_BLOCK, LANES), 1)
    lse_all = jnp.zeros((Q_BLOCK, LANES), F32)
    scale = HEAD_DIM ** -0.5
    for h in range(HEADS_PER_GROUP):
        hs = slice(h * HEAD_DIM, (h + 1) * HEAD_DIM)
        q = rope(q_ref[0, :, hs].astype(F32), cos_c, sin_c) * scale
        kc = rope(kc_ref[0, :, hs].astype(F32), cos_c, sin_c)
        kp = rope(kp_ref[0, :, hs].astype(F32), cos_p, sin_p)
        k = jnp.concatenate([kp, kc], axis=0).astype(BF16)
        v = jnp.concatenate([vp_ref[0, :, hs], vc_ref[0, :, hs]], axis=0)
        s = lax.dot_general(q.astype(BF16), k, (((1,), (1,)), ((), ())),
                            preferred_element_type=F32)
        s = jnp.where(mask, s, -jnp.inf)
        m = jnp.max(s, axis=-1, keepdims=True)
        p = jnp.exp(s - m)
        l = jnp.sum(p, axis=-1, keepdims=True)
        o = jnp.dot(p.astype(BF16), v, preferred_element_type=F32) / l
        o_ref[0, :, hs] = o.astype(o_ref.dtype)
        lse_all = jnp.where(lse_lane == h, m + jnp.log(l), lse_all)
    lse_ref[0] = lse_all


def _attention_group(proj, cos_t, sin_t, batch, seq, dilation, group, col_q, col_k, col_v, n_cols):
    d = dilation
    sub_len = seq // d
    nb = sub_len // Q_BLOCK
    proj_v = proj.reshape(batch, sub_len, d * n_cols)
    cos_v = cos_t.reshape(sub_len, d * HEAD_DIM)
    sin_v = sin_t.reshape(sub_len, d * HEAD_DIM)
    cpb = n_cols // GROUP_WIDTH

    def cur(col):
        return pl.BlockSpec((1, Q_BLOCK, GROUP_WIDTH),
                            lambda b, r, n: (b, n, r * cpb + col // GROUP_WIDTH + group))

    def prev(col):
        return pl.BlockSpec((1, Q_BLOCK, GROUP_WIDTH),
                            lambda b, r, n: (b, jnp.maximum(n - 1, 0), r * cpb + col // GROUP_WIDTH + group))

    tab_c = pl.BlockSpec((Q_BLOCK, HEAD_DIM), lambda b, r, n: (n, r))
    tab_p = pl.BlockSpec((Q_BLOCK, HEAD_DIM), lambda b, r, n: (jnp.maximum(n - 1, 0), r))
    o, lse = pl.pallas_call(
        _attn_kernel,
        grid=(batch, d, nb),
        in_specs=[cur(col_q), cur(col_k), prev(col_k), cur(col_v), prev(col_v),
                  tab_c, tab_c, tab_p, tab_p],
        out_specs=[pl.BlockSpec((1, Q_BLOCK, GROUP_WIDTH), lambda b, r, n: (b, n, r)),
                   pl.BlockSpec((1, Q_BLOCK, LANES), lambda b, r, n: (b, n, r))],
        out_shape=[jax.ShapeDtypeStruct((batch, sub_len, d * GROUP_WIDTH), BF16),
                   jax.ShapeDtypeStruct((batch, sub_len, d * LANES), F32)],
        compiler_params=_params(("parallel", "parallel", "arbitrary"), 32),
        name=f"attn_d{d}",
    )(proj_v, proj_v, proj_v, proj_v, proj_v, cos_v, sin_v, cos_v, sin_v)
    return o.reshape(batch * seq, GROUP_WIDTH), lse.reshape(batch * seq, LANES)


def _rope_tables(seq):
    half = ROT_DIM // 2
    inv_freq = jnp.power(ROPE_THETA, -jnp.arange(half, dtype=F32) * (2.0 / ROT_DIM))
    ang = jnp.arange(seq, dtype=jnp.int32).astype(F32)[:, None] * inv_freq[None, :]
    cos, sin = jnp.cos(ang), jnp.sin(ang)
    rest = HEAD_DIM - ROT_DIM
    cos_t = jnp.concatenate([cos, cos, jnp.ones((seq, rest), F32)], axis=1)
    sin_t = jnp.concatenate([-sin, sin, jnp.zeros((seq, rest), F32)], axis=1)
    return cos_t, sin_t


def _mix_kernel(lru_ref, o1_ref, o2_ref, o3_ref, l1_ref, l2_ref, l3_ref, gl_ref, ga_ref, x_ref,
                wl_ref, wat_ref, wo_ref, g_ref, b_ref, x1_ref, x1p_ref):
    l1, l2, l3 = l1_ref[...], l2_ref[...], l3_ref[...]
    m = jnp.maximum(jnp.maximum(l1, l2), l3)
    e1, e2, e3 = jnp.exp(l1 - m), jnp.exp(l2 - m), jnp.exp(l3 - m)
    inv = 1.0 / (e1 + e2 + e3)
    w1, w2, w3 = e1 * inv, e2 * inv, e3 * inv
    parts = []
    for h in range(HEADS_PER_GROUP):
        hs = slice(h * HEAD_DIM, (h + 1) * HEAD_DIM)
        parts.append(w1[:, h:h + 1] * o1_ref[:, hs].astype(F32)
                     + w2[:, h:h + 1] * o2_ref[:, hs].astype(F32)
                     + w3[:, h:h + 1] * o3_ref[:, hs].astype(F32))
    attn = jnp.concatenate(parts, axis=1).astype(BF16)
    lru_p = jnp.dot(lru_ref[...], wl_ref[...], preferred_element_type=F32)
    att_p = jnp.dot(attn, wat_ref[...], preferred_element_type=F32)
    merged = (_sigmoid(gl_ref[...].astype(F32)) * lru_p
              + _sigmoid(ga_ref[...].astype(F32)) * att_p)
    mix = jnp.dot(merged.astype(BF16), wo_ref[...], preferred_element_type=F32)
    x1 = _layer_norm(ALPHA * x_ref[...] + mix, g_ref[...], b_ref[...])
    x1_ref[...] = x1
    x1p_ref[...] = _pack_halves(x1)


def _mix(lru, outs, lses, proj, x, w_lru, w_attn, w_out, g, b, col_gl, col_ga):
    t_total, d_model = x.shape
    tm = min(256, t_total)
    row = lambda i: (i, 0)
    const = lambda i: (0, 0)
    resident = functools.partial(pl.BlockSpec, index_map=const, pipeline_mode=pl.Buffered(1))
    return pl.pallas_call(
        _mix_kernel,
        grid=(t_total // tm,),
        in_specs=[pl.BlockSpec((tm, lru.shape[1]), row)]
        + [pl.BlockSpec((tm, GROUP_WIDTH), row)] * 3
        + [pl.BlockSpec((tm, LANES), row)] * 3
        + [pl.BlockSpec((tm, d_model), lambda i: (i, col_gl // d_model)),
           pl.BlockSpec((tm, d_model), lambda i: (i, col_ga // d_model)),
           pl.BlockSpec((tm, d_model), row),
           resident(w_lru.shape), resident(w_attn.shape), resident(w_out.shape),
           pl.BlockSpec((1, d_model), const), pl.BlockSpec((1, d_model), const)],
        out_specs=[pl.BlockSpec((tm, d_model), row), pl.BlockSpec((tm, d_model // 2), row)],
        out_shape=[jax.ShapeDtypeStruct((t_total, d_model), F32),
                   jax.ShapeDtypeStruct((t_total, d_model // 2), U32)],
        compiler_params=_params(("parallel",), 56),
        name="mix",
    )(lru, *outs, *lses, proj, proj, x, w_lru, w_attn, w_out, g, b)


def _router_kernel(x_ref, rwt_ref, bias_ref, eidx_ref, rank_ref, gate_ref, cnt_ref, carry_ref,
                   *, tm):
    @pl.when(pl.program_id(0) == 0)
    def _():
        carry_ref[...] = jnp.zeros_like(carry_ref)

    neg_inf = -jnp.inf
    logits = lax.dot_general(rwt_ref[...], x_ref[...], (((1,), (1,)), ((), ())),
                             precision=lax.Precision.HIGHEST, preferred_element_type=F32)
    scores = _sigmoid(logits)
    biased = scores + bias_ref[...]

    sub = lax.broadcasted_iota(I32, (GROUP_SIZE, tm), 0).astype(F32)
    group_score = []
    for g in range(N_EXPERT_GROUPS):
        v = biased[g * GROUP_SIZE:(g + 1) * GROUP_SIZE, :]
        m1 = jnp.max(v, axis=0, keepdims=True)
        first = jnp.min(jnp.where(v == m1, sub, float(GROUP_SIZE)), axis=0, keepdims=True)
        m2 = jnp.max(jnp.where(sub == first, neg_inf, v), axis=0, keepdims=True)
        group_score.append(m1 + m2)

    pieces = []
    for g in range(N_EXPERT_GROUPS):
        beaten_by = jnp.zeros((1, tm), F32)
        for o in range(N_EXPERT_GROUPS):
            if o == g:
                continue
            wins = group_score[o] > group_score[g]
            if o < g:
                wins = wins | (group_score[o] == group_score[g])
            beaten_by = beaten_by + wins.astype(F32)
        keep = jnp.broadcast_to(beaten_by < float(TOPK_GROUPS), (GROUP_SIZE, tm))
        pieces.append(jnp.where(keep, biased[g * GROUP_SIZE:(g + 1) * GROUP_SIZE, :], neg_inf))
    vals = jnp.concatenate(pieces, axis=0)

    row = lax.broadcasted_iota(I32, (N_EXPERTS, tm), 0).astype(F32)
    chosen = []
    selected = jnp.zeros((N_EXPERTS, tm), F32)
    for _ in range(TOP_K):
        m = jnp.max(vals, axis=0, keepdims=True)
        idx = jnp.min(jnp.where(vals == m, row, float(N_EXPERTS)), axis=0, keepdims=True)
        hit = row == idx
        vals = jnp.where(hit, neg_inf, vals)
        selected = jnp.where(hit, 1.0, selected)
        chosen.append(idx)

    before = (lax.broadcasted_iota(I32, (tm, tm), 0) < lax.broadcasted_iota(I32, (tm, tm), 1))
    excl = jnp.dot(selected.astype(BF16), before.astype(BF16), preferred_element_type=F32)
    rank_dense = carry_ref[...] + excl
    carry_ref[...] = carry_ref[...] + jnp.sum(selected, axis=1, keepdims=True)
    cnt_ref[...] = carry_ref[...]

    gates = []
    for k in range(TOP_K):
        hit = row == chosen[k]
        gates.append(jnp.sum(jnp.where(hit, scores, 0.0), axis=0, keepdims=True))
        rank_k = jnp.sum(jnp.where(hit, rank_dense, 0.0), axis=0, keepdims=True)
        eidx_ref[k:k + 1, :] = chosen[k].astype(I32)
        rank_ref[k:k + 1, :] = rank_k.astype(I32)
    total = gates[0]
    for k in range(1, TOP_K):
        total = total + gates[k]
    for k in range(TOP_K):
        gate_ref[k:k + 1, :] = gates[k] / total * ROUTED_SCALE


def _router(x1, router_wt, router_bias):
    t_total, d_model = x1.shape
    tm = min(512, t_total)
    tok = pl.BlockSpec((TOP_K, tm), lambda i: (0, i))
    return pl.pallas_call(
        functools.partial(_router_kernel, tm=tm),
        grid=(t_total // tm,),
        in_specs=[pl.BlockSpec((tm, d_model), lambda i: (i, 0)),
                  pl.BlockSpec((N_EXPERTS, d_model), lambda i: (0, 0)),
                  pl.BlockSpec((N_EXPERTS, 1), lambda i: (0, 0))],
        out_specs=[tok, tok, tok, pl.BlockSpec((N_EXPERTS, 1), lambda i: (0, 0))],
        out_shape=[jax.ShapeDtypeStruct((TOP_K, t_total), I32),
                   jax.ShapeDtypeStruct((TOP_K, t_total), I32),
                   jax.ShapeDtypeStruct((TOP_K, t_total), F32),
                   jax.ShapeDtypeStruct((N_EXPERTS, 1), F32)],
        scratch_shapes=[pltpu.VMEM((N_EXPERTS, 1), F32)],
        compiler_params=_params(("arbitrary",), 32),
        name="router",
    )(x1, router_wt, router_bias)


def _row_copies(dest_ref, tm, make):
    def start(t, carry):
        for k in range(TOP_K):
            make(t, k, dest_ref[k, t]).start()
        return carry

    def wait(t, carry):
        for k in range(TOP_K):
            make(t, k, dest_ref[k, t]).wait()
        return carry

    return start, wait


def _dispatch_kernel(dest_ref, x_ref, zero_ref, xs_ref, sem, *, tm):
    del zero_ref

    def make(t, k, row):
        return pltpu.make_async_copy(x_ref.at[pl.ds(t, 1), :], xs_ref.at[pl.ds(row, 1), :], sem)

    start, wait = _row_copies(dest_ref, tm, make)
    lax.fori_loop(0, tm, start, 0)
    lax.fori_loop(0, tm, wait, 0)


def _dispatch(dest, x1p, n_rows):
    t_total, width = x1p.shape
    tm = min(256, t_total)
    zeros = jnp.zeros((n_rows, width), U32)
    return pl.pallas_call(
        functools.partial(_dispatch_kernel, tm=tm),
        grid=(t_total // tm,),
        in_specs=[pl.BlockSpec((TOP_K, tm), lambda i: (0, i), memory_space=pltpu.SMEM),
                  pl.BlockSpec((tm, width), lambda i: (i, 0)),
                  pl.BlockSpec(memory_space=pl.ANY)],
        out_specs=pl.BlockSpec(memory_space=pl.ANY),
        out_shape=jax.ShapeDtypeStruct((n_rows, width), U32),
        scratch_shapes=[pltpu.SemaphoreType.DMA],
        input_output_aliases={2: 0},
        compiler_params=_params(("arbitrary",), 32),
        name="dispatch",
    )(dest, x1p, zeros)


def _expert_kernel(be_ref, nu_ref, xs_ref, wg_ref, wu_ref, wd_ref, ys_ref):
    del be_ref
    used = pl.program_id(0) < nu_ref[0]

    @pl.when(used)
    def _():
        lo, hi = _unpack_halves(xs_ref[...])
        lo, hi = lo.astype(BF16), hi.astype(BF16)
        half = lo.shape[1]
        wg, wu = wg_ref[0], wu_ref[0]
        g = (jnp.dot(lo, wg[:half], preferred_element_type=F32)
             + jnp.dot(hi, wg[half:], preferred_element_type=F32))
        u = (jnp.dot(lo, wu[:half], preferred_element_type=F32)
             + jnp.dot(hi, wu[half:], preferred_element_type=F32))
        h = (_silu(g) * u).astype(BF16)
        ys_ref[...] = _pack_halves(jnp.dot(h, wd_ref[0], preferred_element_type=F32))

    @pl.when(jnp.logical_not(used))
    def _():
        ys_ref[...] = jnp.zeros_like(ys_ref)


def _experts(block_e, n_used, xs, w_gate, w_up, w_down, bm):
    n_rows, width = xs.shape
    n_blocks = n_rows // bm
    d_model, hidden = w_gate.shape[1], w_gate.shape[2]
    return pl.pallas_call(
        _expert_kernel,
        grid_spec=pltpu.PrefetchScalarGridSpec(
            num_scalar_prefetch=2,
            grid=(n_blocks,),
            in_specs=[
                pl.BlockSpec((bm, width), lambda i, be, nu: (jnp.minimum(i, nu[0] - 1), 0)),
                pl.BlockSpec((1, d_model, hidden), lambda i, be, nu: (be[i], 0, 0)),
                pl.BlockSpec((1, d_model, hidden), lambda i, be, nu: (be[i], 0, 0)),
                pl.BlockSpec((1, hidden, d_model), lambda i, be, nu: (be[i], 0, 0)),
            ],
            out_specs=pl.BlockSpec((bm, width),
                                   lambda i, be, nu: (jnp.where(i < nu[0], i, n_blocks), 0)),
        ),
        out_shape=jax.ShapeDtypeStruct((n_rows + bm, width), U32),
        compiler_params=_params(("arbitrary",), 48),
        name="experts",
    )(block_e, n_used, xs, w_gate, w_up, w_down)


def _combine_kernel(dest_ref, ys_ref, x1_ref, gate_ref, sg_ref, su_ref, sd_ref, g_ref, b_ref,
                    o_ref, ob_ref, buf_ref, sem, *, tm):
    def make(t, k, row):
        return pltpu.make_async_copy(ys_ref.at[pl.ds(row, 1), :], buf_ref.at[k, pl.ds(t, 1), :], sem)

    start, wait = _row_copies(dest_ref, tm, make)
    lax.fori_loop(0, tm, start, 0)

    x1 = x1_ref[...]
    xb = x1.astype(BF16)
    hs = (_silu(jnp.dot(xb, sg_ref[...], preferred_element_type=F32))
          * jnp.dot(xb, su_ref[...], preferred_element_type=F32)).astype(BF16)
    shared = jnp.dot(hs, sd_ref[...], preferred_element_type=F32)

    lax.fori_loop(0, tm, wait, 0)
    gates = gate_ref[...]
    half = buf_ref.shape[2]
    acc_lo = jnp.zeros((tm, half), F32)
    acc_hi = jnp.zeros((tm, half), F32)
    for k in range(TOP_K):
        lo, hi = _unpack_halves(buf_ref[k])
        gk = gates[:, k:k + 1]
        acc_lo = acc_lo + gk * lo
        acc_hi = acc_hi + gk * hi
    ffn = jnp.concatenate([acc_lo, acc_hi], axis=1) + shared
    x2 = _layer_norm(ALPHA * x1 + ffn, g_ref[...], b_ref[...])
    o_ref[...] = x2
    ob_ref[...] = x2.astype(BF16)


def _combine(dest, ys, x1, gates, sh_gate, sh_up, sh_down, g, b):
    t_total, d_model = x1.shape
    tm = min(128, t_total)
    row = lambda i: (i, 0)
    const = lambda i: (0, 0)
    return pl.pallas_call(
        functools.partial(_combine_kernel, tm=tm),
        grid=(t_total // tm,),
        in_specs=[pl.BlockSpec((TOP_K, tm), lambda i: (0, i), memory_space=pltpu.SMEM),
                  pl.BlockSpec(memory_space=pl.ANY),
                  pl.BlockSpec((tm, d_model), row),
                  pl.BlockSpec((tm, TOP_K), row),
                  pl.BlockSpec(sh_gate.shape, const),
                  pl.BlockSpec(sh_up.shape, const),
                  pl.BlockSpec(sh_down.shape, const),
                  pl.BlockSpec((1, d_model), const), pl.BlockSpec((1, d_model), const)],
        out_specs=[pl.BlockSpec((tm, d_model), row), pl.BlockSpec((tm, d_model), row)],
        out_shape=[jax.ShapeDtypeStruct((t_total, d_model), F32),
                   jax.ShapeDtypeStruct((t_total, d_model), BF16)],
        scratch_shapes=[pltpu.VMEM((TOP_K, tm, d_model // 2), U32), pltpu.SemaphoreType.DMA],
        compiler_params=_params(("arbitrary",), 48),
        name="combine",
    )(dest, ys, x1, gates, sh_gate, sh_up, sh_down, g, b)


def _mixer_layer(x, xb, p, batch, seq, cos_t, sin_t):
    d_model = x.shape[1]
    lru_w = p["conv_w"].shape[1]
    attn_w = len(DILATED_GROUPS) * GROUP_WIDTH
    s = [0, lru_w, 2 * lru_w, 2 * lru_w + attn_w, 2 * lru_w + 2 * attn_w, 2 * lru_w + 3 * attn_w,
         2 * lru_w + 3 * attn_w + d_model, 2 * lru_w + 3 * attn_w + 2 * d_model]
    w_in = p["w_in"]
    w_in = jnp.concatenate([w_in[:, s[0]:s[2]], w_in[:, s[5]:s[7]], w_in[:, s[2]:s[5]]],
                           axis=1).astype(BF16)
    col_u, col_y, col_gl, col_ga = 0, lru_w, 2 * lru_w, 2 * lru_w + d_model
    col_q = 2 * lru_w + 2 * d_model
    col_k, col_v = col_q + attn_w, col_q + 2 * attn_w
    n_cols = w_in.shape[1]

    proj = _matmul(xb, w_in, 512, 1280, BF16)
    vec = lambda a: a.reshape(1, -1)
    lru = _lru(proj, p["conv_w"], vec(p["conv_b"]), p["lru_wa"].astype(BF16), vec(p["lru_ba"]),
               p["lru_wx"].astype(BF16), vec(p["lru_bx"]), vec(p["lru_lambda"]),
               batch, seq, lru_w, col_u, col_y)
    outs, lses = [], []
    for g, (window, dilation) in enumerate(DILATED_GROUPS):
        assert window // dilation == Q_BLOCK
        o, l = _attention_group(proj, cos_t, sin_t, batch, seq, dilation, g, col_q, col_k, col_v,
                                n_cols)
        outs.append(o)
        lses.append(l)
    return _mix(lru, outs, lses, proj, x, p["w_lru_proj"].astype(BF16),
                p["w_attn_proj"].astype(BF16), p["w_out"].astype(BF16),
                vec(p["ln1_g"]), vec(p["ln1_b"]), col_gl, col_ga)


def _moe_layer(x1, x1p, p, bm):
    t_total = x1.shape[0]
    eidx, rank, gates, counts = _router(x1, p["router_w"].T, p["router_bias"].reshape(-1, 1))
    counts = counts[:, 0].astype(I32)
    padded = (counts + bm - 1) // bm * bm
    padded_end = jnp.cumsum(padded)
    padded_start = padded_end - padded
    dest = padded_start[eidx] + rank
    n_blocks = t_total * TOP_K // bm + N_EXPERTS
    n_used = (padded_end[-1] // bm).astype(I32).reshape(1)
    block_e = jnp.minimum(jnp.searchsorted(padded_end, jnp.arange(n_blocks, dtype=I32) * bm,
                                           side="right"), N_EXPERTS - 1).astype(I32)
    xs = _dispatch(dest, x1p, n_blocks * bm)
    ys = _experts(block_e, n_used, xs, p["exp_w_gate"].astype(BF16), p["exp_w_up"].astype(BF16),
                  p["exp_w_down"].astype(BF16), bm)
    vec = lambda a: a.reshape(1, -1)
    return _combine(dest, ys, x1, gates.T, p["sh_w_gate"].astype(BF16), p["sh_w_up"].astype(BF16),
                    p["sh_w_down"].astype(BF16), vec(p["ln2_g"]), vec(p["ln2_b"]))


_NAMES = ("w_in", "conv_w", "conv_b", "lru_wa", "lru_ba", "lru_wx", "lru_bx", "lru_lambda",
          "w_lru_proj", "w_attn_proj", "w_out", "ln1_g", "ln1_b", "router_w", "router_bias",
          "exp_w_gate", "exp_w_up", "exp_w_down", "sh_w_gate", "sh_w_up", "sh_w_down",
          "ln2_g", "ln2_b")


@jax.jit
def _forward(x, *weights):
    batch, seq, d_model = x.shape
    cos_t, sin_t = _rope_tables(seq)
    bm = min(512, batch * seq)
    xf = x.reshape(batch * seq, d_model)
    xb = xf.astype(BF16)
    for layer in range(DEPTH):
        p = {name: w[layer] for name, w in zip(_NAMES, weights)}
        x1, x1p = _mixer_layer(xf, xb, p, batch, seq, cos_t, sin_t)
        xf, xb = _moe_layer(x1, x1p, p, bm)
    return xf.reshape(batch, seq, d_model)


def kernel(x, w_in, conv_w, conv_b, lru_wa, lru_ba, lru_wx, lru_bx, lru_lambda, w_lru_proj,
           w_attn_proj, w_out, ln1_g, ln1_b, router_w, router_bias, exp_w_gate, exp_w_up,
           exp_w_down, sh_w_gate, sh_w_up, sh_w_down, ln2_g, ln2_b):
    return _forward(x, w_in, conv_w, conv_b, lru_wa, lru_ba, lru_wx, lru_bx, lru_lambda,
                    w_lru_proj, w_attn_proj, w_out, ln1_g, ln1_b, router_w, router_bias,
                    exp_w_gate, exp_w_up, exp_w_down, sh_w_gate, sh_w_up, sh_w_down, ln2_g, ln2_b)
```

```python
import functools
import math

import jax
import jax.numpy as jnp
from jax import lax
from jax.experimental import pallas as pl
from jax.experimental.pallas import tpu as pltpu

F32 = jnp.float32
BF16 = jnp.bfloat16
U32 = jnp.uint32
I32 = jnp.int32

LANES = 128
SUBLANES = 8

DEPTH = 2
LRU_BLOCK = 128
CONV_WIDTH = 4
LRU_C = 8.0
HEAD_DIM = 128
HEADS_PER_GROUP = 4
DILATED_GROUPS = ((128, 1), (512, 4), (2048, 16))
Q_BLOCK = 128
ROPE_THETA = 500000.0
ROT_DIM = HEAD_DIM // 4
N_EXPERTS = 64
TOP_K = 8
N_EXPERT_GROUPS = 8
GROUP_SIZE = N_EXPERTS // N_EXPERT_GROUPS
TOPK_GROUPS = 4
ROUTED_SCALE = 2.5
LN_EPS = 1e-5
ALPHA = (2 * DEPTH) ** 0.25

GROUP_WIDTH = HEADS_PER_GROUP * HEAD_DIM
HI_MASK = 0xFFFF0000


def _params(sem, vmem_mb):
    return pltpu.CompilerParams(dimension_semantics=sem, vmem_limit_bytes=vmem_mb * 2 ** 20)


def _sigmoid(x):
    return 1.0 / (1.0 + jnp.exp(-x))


def _silu(x):
    return x * _sigmoid(x)


def _layer_norm(y, g, b):
    mu = jnp.mean(y, axis=-1, keepdims=True)
    yc = y - mu
    var = jnp.mean(yc * yc, axis=-1, keepdims=True)
    return yc * lax.rsqrt(var + LN_EPS) * g + b


def _pack_halves(y):
    c = y.shape[1] // 2
    lo = lax.bitcast_convert_type(y[:, :c].astype(BF16).astype(F32), U32)
    hi = lax.bitcast_convert_type(y[:, c:].astype(BF16).astype(F32), U32)
    return (lo >> 16) | (hi & jnp.uint32(HI_MASK))


def _unpack_halves(w):
    lo = lax.bitcast_convert_type(w << 16, F32)
    hi = lax.bitcast_convert_type(w & jnp.uint32(HI_MASK), F32)
    return lo, hi


def _store_token_tiles(ref, words):
    m = words.shape[0]
    for c in range(SUBLANES):
        ref[pl.ds(c, m, stride=SUBLANES), :] = words[:, c * LANES:(c + 1) * LANES]


def _load_token_tiles(ref, m):
    return jnp.concatenate([ref[pl.ds(c, m, stride=SUBLANES), :] for c in range(SUBLANES)], axis=1)


def _matmul_kernel(a_ref, b_ref, o_ref):
    o_ref[...] = jnp.dot(a_ref[...], b_ref[...], preferred_element_type=F32).astype(o_ref.dtype)


def _matmul(a, b, bm, bn, out_dtype, name):
    m, k = a.shape
    n = b.shape[1]
    bm = min(bm, m)
    return pl.pallas_call(
        _matmul_kernel,
        grid=(n // bn, m // bm),
        in_specs=[pl.BlockSpec((bm, k), lambda j, i: (i, 0)),
                  pl.BlockSpec((k, bn), lambda j, i: (0, j))],
        out_specs=pl.BlockSpec((bm, bn), lambda j, i: (i, j)),
        out_shape=jax.ShapeDtypeStruct((m, n), out_dtype),
        compiler_params=_params(("parallel", "arbitrary"), 48),
        name=name,
    )(a, b)


def _lru_kernel(u_ref, y_ref, cw_ref, cb_ref, wa_ref, ba_ref, wx_ref, bx_ref, lam_ref,
                o_ref, tail_ref, h_ref, *, tc, nc):
    @pl.when(pl.program_id(2) == 0)
    def _():
        tail_ref[...] = jnp.zeros_like(tail_ref)
        h_ref[...] = jnp.zeros_like(h_ref)

    row8 = lax.broadcasted_iota(I32, (SUBLANES, LANES), 0)
    ones8 = jnp.ones((SUBLANES, LANES), F32)
    zeros8 = jnp.zeros((SUBLANES, LANES), F32)

    def shift_small(x, head8, j):
        r = pltpu.roll(x, j, 0)
        first = jnp.where(row8 < j, pltpu.roll(head8, j, 0), r[:SUBLANES])
        return jnp.concatenate([first, r[SUBLANES:]], axis=0)

    def shift_big(x, fill, d):
        return jnp.concatenate([jnp.full((d, LANES), fill, F32), x[:tc - d]], axis=0)

    for c in range(nc):
        cs = slice(c * LANES, (c + 1) * LANES)
        u = u_ref[:, cs].astype(F32)
        tail = tail_ref[:, cs]
        cw = cw_ref[:, cs]
        uc = u * cw[CONV_WIDTH - 1:CONV_WIDTH, :] + cb_ref[:, cs]
        for j in range(1, CONV_WIDTH):
            uc = uc + shift_small(u, tail, j) * cw[CONV_WIDTH - 1 - j:CONV_WIDTH - j, :]
        tail_ref[:, cs] = u[tc - SUBLANES:, :]

        ub = uc.astype(BF16)
        r = _sigmoid(jnp.dot(ub, wa_ref[c], preferred_element_type=F32) + ba_ref[:, cs])
        i = _sigmoid(jnp.dot(ub, wx_ref[c], preferred_element_type=F32) + bx_ref[:, cs])
        z = -lam_ref[:, cs]
        softplus = jnp.maximum(z, 0.0) + jnp.log(1.0 + jnp.exp(-jnp.abs(z)))
        a = jnp.exp((-LRU_C * softplus) * r)
        b = jnp.sqrt(1.0 - a * a) * (i * uc)

        d = 1
        while d < tc:
            if d < SUBLANES:
                a_sh = shift_small(a, ones8, d)
                b_sh = shift_small(b, zeros8, d)
            else:
                a_sh = shift_big(a, 1.0, d)
                b_sh = shift_big(b, 0.0, d)
            b = a * b_sh + b
            a = a * a_sh
            d *= 2
        h = b + a * h_ref[:, cs]
        h_ref[:, cs] = h[tc - 1:tc, :]

        y = y_ref[:, cs].astype(F32)
        gelu = 0.5 * y * (1.0 + jnp.tanh(math.sqrt(2.0 / math.pi) * (y + 0.044715 * (y * y * y))))
        o_ref[:, cs] = (h * gelu).astype(o_ref.dtype)


def _lru(proj, conv_w, conv_b, wa, ba, wx, bx, lam, batch, seq, width, col_u, col_y):
    tc, nc = 256, 4
    tc = min(tc, seq)
    cw = nc * LANES
    nt = seq // tc
    t_total = batch * seq
    vec = pl.BlockSpec((1, cw), lambda b, c, t: (0, c))
    gate = pl.BlockSpec((nc, LRU_BLOCK, LRU_BLOCK), lambda b, c, t: (c, 0, 0))
    return pl.pallas_call(
        functools.partial(_lru_kernel, tc=tc, nc=nc),
        grid=(batch, width // cw, nt),
        in_specs=[pl.BlockSpec((tc, cw), lambda b, c, t: (b * nt + t, col_u // cw + c)),
                  pl.BlockSpec((tc, cw), lambda b, c, t: (b * nt + t, col_y // cw + c)),
                  pl.BlockSpec((CONV_WIDTH, cw), lambda b, c, t: (0, c)),
                  vec, gate, vec, gate, vec, vec],
        out_specs=pl.BlockSpec((tc, cw), lambda b, c, t: (b * nt + t, c)),
        out_shape=jax.ShapeDtypeStruct((t_total, width), BF16),
        scratch_shapes=[pltpu.VMEM((SUBLANES, cw), F32), pltpu.VMEM((1, cw), F32)],
        compiler_params=_params(("parallel", "parallel", "arbitrary"), 32),
        name="lru",
    )(proj, proj, conv_w, conv_b, wa, ba, wx, bx, lam)


def _attn_kernel(q_ref, kc_ref, kp_ref, vc_ref, vp_ref, cc_ref, sc_ref, cp_ref, sp_ref,
                 o_ref, lse_ref):
    n = pl.program_id(2)
    lane = lax.broadcasted_iota(I32, (Q_BLOCK, HEAD_DIM), 1)
    half = ROT_DIM // 2

    def rope(t, cos, sin):
        swapped = jnp.where(lane < half, pltpu.roll(t, HEAD_DIM - half, 1), pltpu.roll(t, half, 1))
        return t * cos + swapped * sin

    cos_c, sin_c = cc_ref[...], sc_ref[...]
    cos_p, sin_p = cp_ref[...], sp_ref[...]
    qi = lax.broadcasted_iota(I32, (Q_BLOCK, 2 * Q_BLOCK), 0)
    ki = lax.broadcasted_iota(I32, (Q_BLOCK, 2 * Q_BLOCK), 1)
    delta = qi + Q_BLOCK - ki
    mask = (delta >= 0) & (delta <= Q_BLOCK) & jnp.logical_not((n == 0) & (ki < Q_BLOCK))
    lse_lane = lax.broadcasted_iota(I32, (Q_BLOCK, LANES), 1)
    lse_all = jnp.zeros((Q_BLOCK, LANES), F32)
    scale = HEAD_DIM ** -0.5
    for h in range(HEADS_PER_GROUP):
        hs = slice(h * HEAD_DIM, (h + 1) * HEAD_DIM)
        q = rope(q_ref[0, :, hs].astype(F32), cos_c, sin_c) * scale
        kc = rope(kc_ref[0, :, hs].astype(F32), cos_c, sin_c)
        kp = rope(kp_ref[0, :, hs].astype(F32), cos_p, sin_p)
        k = jnp.concatenate([kp, kc], axis=0).astype(BF16)
        v = jnp.concatenate([vp_ref[0, :, hs], vc_ref[0, :, hs]], axis=0)
        s = lax.dot_general(q.astype(BF16), k, (((1,), (1,)), ((), ())),
                            preferred_element_type=F32)
        s = jnp.where(mask, s, -jnp.inf)
        m = jnp.max(s, axis=-1, keepdims=True)
        p = jnp.exp(s - m)
        l = jnp.sum(p, axis=-1, keepdims=True)
        o = jnp.dot(p.astype(BF16), v, preferred_element_type=F32) / l
        o_ref[0, :, hs] = o.astype(o_ref.dtype)
        lse_all = jnp.where(lse_lane == h, m + jnp.log(l), lse_all)
    lse_ref[0] = lse_all


def _attention_group(qkv, cos_t, sin_t, batch, seq, dilation):
    d = dilation
    sub_len = seq // d
    nb = sub_len // Q_BLOCK
    cpb = 3
    proj_v = qkv.reshape(batch, sub_len, d * cpb * GROUP_WIDTH)
    cos_v = cos_t.reshape(sub_len, d * HEAD_DIM)
    sin_v = sin_t.reshape(sub_len, d * HEAD_DIM)
    col_q, col_k, col_v = 0, 1, 2

    def cur(col):
        return pl.BlockSpec((1, Q_BLOCK, GROUP_WIDTH), lambda b, r, n: (b, n, r * cpb + col))

    def prev(col):
        return pl.BlockSpec((1, Q_BLOCK, GROUP_WIDTH),
                            lambda b, r, n: (b, jnp.maximum(n - 1, 0), r * cpb + col))

    tab_c = pl.BlockSpec((Q_BLOCK, HEAD_DIM), lambda b, r, n: (n, r))
    tab_p = pl.BlockSpec((Q_BLOCK, HEAD_DIM), lambda b, r, n: (jnp.maximum(n - 1, 0), r))
    o, lse = pl.pallas_call(
        _attn_kernel,
        grid=(batch, d, nb),
        in_specs=[cur(col_q), cur(col_k), prev(col_k), cur(col_v), prev(col_v),
                  tab_c, tab_c, tab_p, tab_p],
        out_specs=[pl.BlockSpec((1, Q_BLOCK, GROUP_WIDTH), lambda b, r, n: (b, n, r)),
                   pl.BlockSpec((1, Q_BLOCK, LANES), lambda b, r, n: (b, n, r))],
        out_shape=[jax.ShapeDtypeStruct((batch, sub_len, d * GROUP_WIDTH), BF16),
                   jax.ShapeDtypeStruct((batch, sub_len, d * LANES), F32)],
        compiler_params=_params(("parallel", "parallel", "arbitrary"), 32),
        name=f"attn_d{d}",
    )(proj_v, proj_v, proj_v, proj_v, proj_v, cos_v, sin_v, cos_v, sin_v)
    return o.reshape(batch * seq, GROUP_WIDTH), lse.reshape(batch * seq, LANES)


def _rope_tables(seq):
    half = ROT_DIM // 2
    inv_freq = jnp.power(ROPE_THETA, -jnp.arange(half, dtype=F32) * (2.0 / ROT_DIM))
    ang = jnp.arange(seq, dtype=jnp.int32).astype(F32)[:, None] * inv_freq[None, :]
    cos, sin = jnp.cos(ang), jnp.sin(ang)
    rest = HEAD_DIM - ROT_DIM
    cos_t = jnp.concatenate([cos, cos, jnp.ones((seq, rest), F32)], axis=1)
    sin_t = jnp.concatenate([-sin, sin, jnp.zeros((seq, rest), F32)], axis=1)
    return cos_t, sin_t


def _mix_kernel(lru_ref, o1_ref, o2_ref, o3_ref, l1_ref, l2_ref, l3_ref, gl_ref, ga_ref, x_ref,
                wl_ref, wat_ref, wo_ref, g_ref, b_ref, x1_ref, x1p_ref):
    l1, l2, l3 = l1_ref[...], l2_ref[...], l3_ref[...]
    m = jnp.maximum(jnp.maximum(l1, l2), l3)
    e1, e2, e3 = jnp.exp(l1 - m), jnp.exp(l2 - m), jnp.exp(l3 - m)
    inv = 1.0 / (e1 + e2 + e3)
    w1, w2, w3 = e1 * inv, e2 * inv, e3 * inv
    parts = []
    for h in range(HEADS_PER_GROUP):
        hs = slice(h * HEAD_DIM, (h + 1) * HEAD_DIM)
        parts.append(w1[:, h:h + 1] * o1_ref[:, hs].astype(F32)
                     + w2[:, h:h + 1] * o2_ref[:, hs].astype(F32)
                     + w3[:, h:h + 1] * o3_ref[:, hs].astype(F32))
    attn = jnp.concatenate(parts, axis=1).astype(BF16)
    lru_p = jnp.dot(lru_ref[...], wl_ref[...], preferred_element_type=F32)
    att_p = jnp.dot(attn, wat_ref[...], preferred_element_type=F32)
    merged = (_sigmoid(gl_ref[...].astype(F32)) * lru_p
              + _sigmoid(ga_ref[...].astype(F32)) * att_p)
    mix = jnp.dot(merged.astype(BF16), wo_ref[...], preferred_element_type=F32)
    x1 = _layer_norm(ALPHA * x_ref[...] + mix, g_ref[...], b_ref[...])
    x1_ref[...] = x1
    _store_token_tiles(x1p_ref, _pack_halves(x1))


def _mix(lru, outs, lses, proj, x, w_lru, w_attn, w_out, g, b):
    t_total, d_model = x.shape
    col_gl, col_ga = 0, d_model
    tm = min(256, t_total)
    row = lambda i: (i, 0)
    const = lambda i: (0, 0)
    resident = functools.partial(pl.BlockSpec, index_map=const, pipeline_mode=pl.Buffered(1))
    return pl.pallas_call(
        _mix_kernel,
        grid=(t_total // tm,),
        in_specs=[pl.BlockSpec((tm, lru.shape[1]), row)]
        + [pl.BlockSpec((tm, GROUP_WIDTH), row)] * 3
        + [pl.BlockSpec((tm, LANES), row)] * 3
        + [pl.BlockSpec((tm, d_model), lambda i: (i, col_gl // d_model)),
           pl.BlockSpec((tm, d_model), lambda i: (i, col_ga // d_model)),
           pl.BlockSpec((tm, d_model), row),
           resident(w_lru.shape), resident(w_attn.shape), resident(w_out.shape),
           pl.BlockSpec((1, d_model), const), pl.BlockSpec((1, d_model), const)],
        out_specs=[pl.BlockSpec((tm, d_model), row), pl.BlockSpec((tm * SUBLANES, LANES), row)],
        out_shape=[jax.ShapeDtypeStruct((t_total, d_model), F32),
                   jax.ShapeDtypeStruct((t_total * SUBLANES, LANES), U32)],
        compiler_params=_params(("parallel",), 56),
        name="mix",
    )(lru, *outs, *lses, proj, proj, x, w_lru, w_attn, w_out, g, b)


def _router_kernel(x_ref, rwt_ref, bias_ref, eidx_ref, rank_ref, gate_ref, cnt_ref, carry_ref,
                   *, tm):
    @pl.when(pl.program_id(0) == 0)
    def _():
        carry_ref[...] = jnp.zeros_like(carry_ref)

    neg_inf = -jnp.inf
    logits = lax.dot_general(rwt_ref[...], x_ref[...], (((1,), (1,)), ((), ())),
                             precision=lax.Precision.HIGHEST, preferred_element_type=F32)
    scores = _sigmoid(logits)
    biased = scores + bias_ref[...]

    sub = lax.broadcasted_iota(I32, (GROUP_SIZE, tm), 0).astype(F32)
    group_score = []
    for g in range(N_EXPERT_GROUPS):
        v = biased[g * GROUP_SIZE:(g + 1) * GROUP_SIZE, :]
        m1 = jnp.max(v, axis=0, keepdims=True)
        first = jnp.min(jnp.where(v == m1, sub, float(GROUP_SIZE)), axis=0, keepdims=True)
        m2 = jnp.max(jnp.where(sub == first, neg_inf, v), axis=0, keepdims=True)
        group_score.append(m1 + m2)

    pieces = []
    for g in range(N_EXPERT_GROUPS):
        beaten_by = jnp.zeros((1, tm), F32)
        for o in range(N_EXPERT_GROUPS):
            if o == g:
                continue
            wins = group_score[o] > group_score[g]
            if o < g:
                wins = wins | (group_score[o] == group_score[g])
            beaten_by = beaten_by + wins.astype(F32)
        keep = jnp.broadcast_to(beaten_by < float(TOPK_GROUPS), (GROUP_SIZE, tm))
        pieces.append(jnp.where(keep, biased[g * GROUP_SIZE:(g + 1) * GROUP_SIZE, :], neg_inf))
    vals = jnp.concatenate(pieces, axis=0)

    row = lax.broadcasted_iota(I32, (N_EXPERTS, tm), 0).astype(F32)
    chosen = []
    selected = jnp.zeros((N_EXPERTS, tm), F32)
    for _ in range(TOP_K):
        m = jnp.max(vals, axis=0, keepdims=True)
        idx = jnp.min(jnp.where(vals == m, row, float(N_EXPERTS)), axis=0, keepdims=True)
        hit = row == idx
        vals = jnp.where(hit, neg_inf, vals)
        selected = jnp.where(hit, 1.0, selected)
        chosen.append(idx)

    before = (lax.broadcasted_iota(I32, (tm, tm), 0) < lax.broadcasted_iota(I32, (tm, tm), 1))
    excl = jnp.dot(selected.astype(BF16), before.astype(BF16), preferred_element_type=F32)
    rank_dense = carry_ref[...] + excl
    carry_ref[...] = carry_ref[...] + jnp.sum(selected, axis=1, keepdims=True)
    cnt_ref[...] = carry_ref[...]

    gates = []
    for k in range(TOP_K):
        hit = row == chosen[k]
        gates.append(jnp.sum(jnp.where(hit, scores, 0.0), axis=0, keepdims=True))
        rank_k = jnp.sum(jnp.where(hit, rank_dense, 0.0), axis=0, keepdims=True)
        eidx_ref[k:k + 1, :] = chosen[k].astype(I32)
        rank_ref[k:k + 1, :] = rank_k.astype(I32)
    total = gates[0]
    for k in range(1, TOP_K):
        total = total + gates[k]
    for k in range(TOP_K):
        gate_ref[k:k + 1, :] = gates[k] / total * ROUTED_SCALE


def _router(x1, router_wt, router_bias):
    t_total, d_model = x1.shape
    tm = min(512, t_total)
    tok = pl.BlockSpec((TOP_K, tm), lambda i: (0, i))
    return pl.pallas_call(
        functools.partial(_router_kernel, tm=tm),
        grid=(t_total // tm,),
        in_specs=[pl.BlockSpec((tm, d_model), lambda i: (i, 0)),
                  pl.BlockSpec((N_EXPERTS, d_model), lambda i: (0, 0)),
                  pl.BlockSpec((N_EXPERTS, 1), lambda i: (0, 0))],
        out_specs=[tok, tok, tok, pl.BlockSpec((N_EXPERTS, 1), lambda i: (0, 0))],
        out_shape=[jax.ShapeDtypeStruct((TOP_K, t_total), I32),
                   jax.ShapeDtypeStruct((TOP_K, t_total), I32),
                   jax.ShapeDtypeStruct((TOP_K, t_total), F32),
                   jax.ShapeDtypeStruct((N_EXPERTS, 1), F32)],
        scratch_shapes=[pltpu.VMEM((N_EXPERTS, 1), F32)],
        compiler_params=_params(("arbitrary",), 32),
        name="router",
    )(x1, router_wt, router_bias)


def _tile_rows(first_row):
    return pl.ds(pl.multiple_of(first_row, SUBLANES), SUBLANES)


def _row_copies(dest_ref, make):
    def start(t, carry):
        for k in range(TOP_K):
            make(t, k, dest_ref[t * TOP_K + k]).start(priority=k % 2)
        return carry

    def wait(t, carry):
        for k in range(TOP_K):
            make(t, k, dest_ref[t * TOP_K + k]).wait()
        return carry

    return start, wait


def _dispatch_kernel(zstart_ref, dest_ref, x_ref, xs_ref, zero_ref, sem, *, tm, bm):
    @pl.when(pl.program_id(0) == 0)
    def _():
        zero_ref[...] = jnp.zeros_like(zero_ref)

        def fill(e, carry):
            rows = pl.ds(pl.multiple_of(zstart_ref[e], SUBLANES), bm * SUBLANES)
            cp = pltpu.make_async_copy(zero_ref, xs_ref.at[rows, :], sem)
            cp.start()
            cp.wait()
            return carry

        lax.fori_loop(0, N_EXPERTS, fill, 0)

    def make(t, k, row):
        return pltpu.make_async_copy(x_ref.at[_tile_rows(t * SUBLANES), :],
                                     xs_ref.at[_tile_rows(row), :], sem)

    start, wait = _row_copies(dest_ref, make)
    lax.fori_loop(0, tm, start, 0)
    lax.fori_loop(0, tm, wait, 0)


def _dispatch(zstart, dest, x1p, n_blocks, bm):
    t_total = x1p.shape[0] // SUBLANES
    tm = min(256, t_total)
    return pl.pallas_call(
        functools.partial(_dispatch_kernel, tm=tm, bm=bm),
        grid_spec=pltpu.PrefetchScalarGridSpec(
            num_scalar_prefetch=1,
            grid=(t_total // tm,),
            in_specs=[pl.BlockSpec((tm * TOP_K,), lambda i, zs: (i,), memory_space=pltpu.SMEM),
                      pl.BlockSpec((tm * SUBLANES, LANES), lambda i, zs: (i, 0))],
            out_specs=pl.BlockSpec(memory_space=pl.ANY),
            scratch_shapes=[pltpu.VMEM((bm * SUBLANES, LANES), U32), pltpu.SemaphoreType.DMA],
        ),
        out_shape=jax.ShapeDtypeStruct(((n_blocks + 1) * bm * SUBLANES, LANES), U32),
        compiler_params=_params(("arbitrary",), 32),
        name="dispatch",
    )(zstart, dest, x1p)


def _expert_kernel(be_ref, nu_ref, new_ref, xs_ref, wg_ref, wu_ref, wd_ref, ys_ref,
                   wg_s, wu_s, wd_s, *, bm):
    del be_ref
    i = pl.program_id(0)
    used = i < nu_ref[0]

    @pl.when(used & (new_ref[i] == 1))
    def _():
        wg_s[...] = wg_ref[0].astype(BF16)
        wu_s[...] = wu_ref[0].astype(BF16)
        wd_s[...] = wd_ref[0].astype(BF16)

    @pl.when(used)
    def _():
        lo, hi = _unpack_halves(_load_token_tiles(xs_ref, bm))
        lo, hi = lo.astype(BF16), hi.astype(BF16)
        half = lo.shape[1]
        g = (jnp.dot(lo, wg_s[:half, :], preferred_element_type=F32)
             + jnp.dot(hi, wg_s[half:, :], preferred_element_type=F32))
        u = (jnp.dot(lo, wu_s[:half, :], preferred_element_type=F32)
             + jnp.dot(hi, wu_s[half:, :], preferred_element_type=F32))
        h = (_silu(g) * u).astype(BF16)
        y = jnp.dot(h, wd_s[...], preferred_element_type=F32)
        _store_token_tiles(ys_ref, _pack_halves(y))

    @pl.when(jnp.logical_not(used))
    def _():
        ys_ref[...] = jnp.zeros_like(ys_ref)


def _experts(block_e, n_used, new_expert, xs, w_gate, w_up, w_down, bm):
    block_rows = bm * SUBLANES
    n_blocks = xs.shape[0] // block_rows - 1
    d_model, hidden = w_gate.shape[1], w_gate.shape[2]
    expert = lambda i, be, nu, ne: (be[i], 0, 0)
    return pl.pallas_call(
        functools.partial(_expert_kernel, bm=bm),
        grid_spec=pltpu.PrefetchScalarGridSpec(
            num_scalar_prefetch=3,
            grid=(n_blocks,),
            in_specs=[
                pl.BlockSpec((block_rows, LANES),
                             lambda i, be, nu, ne: (jnp.minimum(i, nu[0] - 1), 0)),
                pl.BlockSpec((1, d_model, hidden), expert),
                pl.BlockSpec((1, d_model, hidden), expert),
                pl.BlockSpec((1, hidden, d_model), expert),
            ],
            out_specs=pl.BlockSpec((block_rows, LANES),
                                   lambda i, be, nu, ne: (jnp.where(i < nu[0], i, n_blocks), 0)),
            scratch_shapes=[pltpu.VMEM((d_model, hidden), BF16), pltpu.VMEM((d_model, hidden), BF16),
                            pltpu.VMEM((hidden, d_model), BF16)],
        ),
        out_shape=jax.ShapeDtypeStruct(((n_blocks + 1) * block_rows, LANES), U32),
        compiler_params=_params(("arbitrary",), 56),
        name="experts",
    )(block_e, n_used, new_expert, xs, w_gate, w_up, w_down)


def _combine_kernel(dest_ref, ys_ref, x1_ref, gate_ref, sg_ref, su_ref, sd_ref, g_ref, b_ref,
                    o_ref, ob_ref, buf_ref, sem, *, tm):
    def make(t, k, row):
        return pltpu.make_async_copy(ys_ref.at[_tile_rows(row), :],
                                     buf_ref.at[k, _tile_rows(t * SUBLANES), :], sem)

    start, wait = _row_copies(dest_ref, make)
    lax.fori_loop(0, tm, start, 0)

    x1 = x1_ref[...]
    xb = x1.astype(BF16)
    hs = (_silu(jnp.dot(xb, sg_ref[...], preferred_element_type=F32))
          * jnp.dot(xb, su_ref[...], preferred_element_type=F32)).astype(BF16)
    shared = jnp.dot(hs, sd_ref[...], preferred_element_type=F32)

    lax.fori_loop(0, tm, wait, 0)
    gates = gate_ref[...]
    half = SUBLANES * LANES
    acc_lo = jnp.zeros((tm, half), F32)
    acc_hi = jnp.zeros((tm, half), F32)
    for k in range(TOP_K):
        lo, hi = _unpack_halves(_load_token_tiles(buf_ref.at[k], tm))
        gk = gates[:, k:k + 1]
        acc_lo = acc_lo + gk * lo
        acc_hi = acc_hi + gk * hi
    ffn = jnp.concatenate([acc_lo, acc_hi], axis=1) + shared
    x2 = _layer_norm(ALPHA * x1 + ffn, g_ref[...], b_ref[...])
    o_ref[...] = x2
    ob_ref[...] = x2.astype(BF16)


def _combine(dest, ys, x1, gates, sh_gate, sh_up, sh_down, g, b):
    t_total, d_model = x1.shape
    tm = min(256, t_total)
    row = lambda i: (i, 0)
    const = lambda i: (0, 0)
    resident = functools.partial(pl.BlockSpec, index_map=const, pipeline_mode=pl.Buffered(1))
    return pl.pallas_call(
        functools.partial(_combine_kernel, tm=tm),
        grid=(t_total // tm,),
        in_specs=[pl.BlockSpec((tm * TOP_K,), lambda i: (i,), memory_space=pltpu.SMEM),
                  pl.BlockSpec(memory_space=pl.ANY),
                  pl.BlockSpec((tm, d_model), row),
                  pl.BlockSpec((tm, TOP_K), row),
                  resident(sh_gate.shape), resident(sh_up.shape), resident(sh_down.shape),
                  pl.BlockSpec((1, d_model), const), pl.BlockSpec((1, d_model), const)],
        out_specs=[pl.BlockSpec((tm, d_model), row), pl.BlockSpec((tm, d_model), row)],
        out_shape=[jax.ShapeDtypeStruct((t_total, d_model), F32),
                   jax.ShapeDtypeStruct((t_total, d_model), BF16)],
        scratch_shapes=[pltpu.VMEM((TOP_K, tm * SUBLANES, LANES), U32), pltpu.SemaphoreType.DMA],
        compiler_params=_params(("arbitrary",), 48),
        name="combine",
    )(dest, ys, x1, gates, sh_gate, sh_up, sh_down, g, b)


def _mixer_layer(x, xb, p, batch, seq, cos_t, sin_t):
    d_model = x.shape[1]
    lru_w = p["conv_w"].shape[1]
    attn_w = len(DILATED_GROUPS) * GROUP_WIDTH
    col_q, col_k, col_v = 2 * lru_w, 2 * lru_w + attn_w, 2 * lru_w + 2 * attn_w
    col_g = 2 * lru_w + 3 * attn_w
    w_in = p["w_in"]
    proj_uy = _matmul(xb, w_in[:, :col_q].astype(BF16), 1024, 1024, BF16, "in_proj_lru")
    proj_g = _matmul(xb, w_in[:, col_g:].astype(BF16), 1024, 1024, BF16, "in_proj_gates")

    vec = lambda a: a.reshape(1, -1)
    lru = _lru(proj_uy, p["conv_w"], vec(p["conv_b"]), p["lru_wa"].astype(BF16), vec(p["lru_ba"]),
               p["lru_wx"].astype(BF16), vec(p["lru_bx"]), vec(p["lru_lambda"]),
               batch, seq, lru_w, 0, lru_w)
    outs, lses = [], []
    for g, (window, dilation) in enumerate(DILATED_GROUPS):
        assert window // dilation == Q_BLOCK
        heads = lambda col: w_in[:, col + g * GROUP_WIDTH:col + (g + 1) * GROUP_WIDTH]
        w_qkv = jnp.concatenate([heads(col_q), heads(col_k), heads(col_v)], axis=1).astype(BF16)
        qkv = _matmul(xb, w_qkv, 1024, 3 * GROUP_WIDTH, BF16, f"in_proj_qkv{g}")
        o, l = _attention_group(qkv, cos_t, sin_t, batch, seq, dilation)
        outs.append(o)
        lses.append(l)
    return _mix(lru, outs, lses, proj_g, x, p["w_lru_proj"].astype(BF16),
                p["w_attn_proj"].astype(BF16), p["w_out"].astype(BF16),
                vec(p["ln1_g"]), vec(p["ln1_b"]))


def _moe_layer(x1, x1p, p, bm):
    t_total = x1.shape[0]
    eidx, rank, gates, counts = _router(x1, p["router_w"].T, p["router_bias"].reshape(-1, 1))
    counts = counts[:, 0].astype(I32)
    padded = (counts + bm - 1) // bm * bm
    padded_end = jnp.cumsum(padded)
    padded_start = padded_end - padded
    experts = jnp.arange(N_EXPERTS, dtype=I32)
    dest = rank + jnp.sum(jnp.where(eidx[None] == experts[:, None, None],
                                    padded_start[:, None, None], 0), axis=0)
    n_blocks = t_total * TOP_K // bm + N_EXPERTS
    n_used = (padded_end[-1] // bm).astype(I32).reshape(1)
    block_row = jnp.arange(n_blocks, dtype=I32) * bm
    block_e = jnp.minimum(jnp.sum((padded_end[None, :] <= block_row[:, None]).astype(I32), axis=1),
                          N_EXPERTS - 1)
    new_expert = jnp.concatenate([jnp.ones((1,), I32),
                                  (block_e[1:] != block_e[:-1]).astype(I32)])
    dest = (dest * SUBLANES).T.reshape(-1)
    xs = _dispatch((padded_start + counts) * SUBLANES, dest, x1p, n_blocks, bm)
    ys = _experts(block_e, n_used, new_expert, xs, p["exp_w_gate"], p["exp_w_up"],
                  p["exp_w_down"], bm)
    vec = lambda a: a.reshape(1, -1)
    return _combine(dest, ys, x1, gates.T, p["sh_w_gate"].astype(BF16), p["sh_w_up"].astype(BF16),
                    p["sh_w_down"].astype(BF16), vec(p["ln2_g"]), vec(p["ln2_b"]))


_NAMES = ("w_in", "conv_w", "conv_b", "lru_wa", "lru_ba", "lru_wx", "lru_bx", "lru_lambda",
          "w_lru_proj", "w_attn_proj", "w_out", "ln1_g", "ln1_b", "router_w", "router_bias",
          "exp_w_gate", "exp_w_up", "exp_w_down", "sh_w_gate", "sh_w_up", "sh_w_down",
          "ln2_g", "ln2_b")


@jax.jit
def _forward(x, *weights):
    batch, seq, d_model = x.shape
    cos_t, sin_t = _rope_tables(seq)
    bm = min(512, batch * seq)
    xf = x.reshape(batch * seq, d_model)
    xb = xf.astype(BF16)
    for layer in range(DEPTH):
        p = {name: w[layer] for name, w in zip(_NAMES, weights)}
        x1, x1p = _mixer_layer(xf, xb, p, batch, seq, cos_t, sin_t)
        xf, xb = _moe_layer(x1, x1p, p, bm)
    return xf.reshape(batch, seq, d_model)


def kernel(x, w_in, conv_w, conv_b, lru_wa, lru_ba, lru_wx, lru_bx, lru_lambda, w_lru_proj,
           w_attn_proj, w_out, ln1_g, ln1_b, router_w, router_bias, exp_w_gate, exp_w_up,
           exp_w_down, sh_w_gate, sh_w_up, sh_w_down, ln2_g, ln2_b):
    return _forward(x, w_in, conv_w, conv_b, lru_wa, lru_ba, lru_wx, lru_bx, lru_lambda,
                    w_lru_proj, w_attn_proj, w_out, ln1_g, ln1_b, router_w, router_bias,
                    exp_w_gate, exp_w_up, exp_w_down, sh_w_gate, sh_w_up, sh_w_down, ln2_g, ln2_b)
```

```python
import functools
import math

import jax
import jax.numpy as jnp
from jax import lax
from jax.experimental import pallas as pl
from jax.experimental.pallas import tpu as pltpu

F32 = jnp.float32
BF16 = jnp.bfloat16
U32 = jnp.uint32
I32 = jnp.int32

LANES = 128
SUBLANES = 8

DEPTH = 2
LRU_BLOCK = 128
CONV_WIDTH = 4
LRU_C = 8.0
HEAD_DIM = 128
HEADS_PER_GROUP = 4
DILATED_GROUPS = ((128, 1), (512, 4), (2048, 16))
Q_BLOCK = 128
ROPE_THETA = 500000.0
ROT_DIM = HEAD_DIM // 4
N_EXPERTS = 64
TOP_K = 8
N_EXPERT_GROUPS = 8
GROUP_SIZE = N_EXPERTS // N_EXPERT_GROUPS
TOPK_GROUPS = 4
ROUTED_SCALE = 2.5
LN_EPS = 1e-5
ALPHA = (2 * DEPTH) ** 0.25

GROUP_WIDTH = HEADS_PER_GROUP * HEAD_DIM
HI_MASK = 0xFFFF0000


def _params(sem, vmem_mb):
    return pltpu.CompilerParams(dimension_semantics=sem, vmem_limit_bytes=vmem_mb * 2 ** 20)


def _sigmoid(x):
    return 0.5 * jnp.tanh(0.5 * x) + 0.5


def _sigmoid_exp(x):
    return 1.0 / (1.0 + jnp.exp(-x))


def _silu(x):
    return x * _sigmoid(x)


def _layer_norm(y, g, b):
    mu = jnp.mean(y, axis=-1, keepdims=True)
    yc = y - mu
    var = jnp.mean(yc * yc, axis=-1, keepdims=True)
    return yc * lax.rsqrt(var + LN_EPS) * g + b


def _pack_halves(y):
    c = y.shape[1] // 2
    lo = lax.bitcast_convert_type(y[:, :c].astype(BF16).astype(F32), U32)
    hi = lax.bitcast_convert_type(y[:, c:].astype(BF16).astype(F32), U32)
    return (lo >> 16) | (hi & jnp.uint32(HI_MASK))


def _unpack_halves(w):
    lo = lax.bitcast_convert_type(w << 16, F32)
    hi = lax.bitcast_convert_type(w & jnp.uint32(HI_MASK), F32)
    return lo, hi


def _store_token_tiles(ref, words):
    m = words.shape[0]
    for c in range(SUBLANES):
        ref[pl.ds(c, m, stride=SUBLANES), :] = words[:, c * LANES:(c + 1) * LANES]


def _load_token_tiles(ref, m):
    return jnp.concatenate([ref[pl.ds(c, m, stride=SUBLANES), :] for c in range(SUBLANES)], axis=1)


def _matmul_kernel(a_ref, b_ref, o_ref):
    o_ref[...] = jnp.dot(a_ref[...], b_ref[...], preferred_element_type=F32).astype(o_ref.dtype)


def _matmul(a, b, bm, bn, out_dtype, name):
    m, k = a.shape
    n = b.shape[1]
    bm = min(bm, m)
    return pl.pallas_call(
        _matmul_kernel,
        grid=(n // bn, m // bm),
        in_specs=[pl.BlockSpec((bm, k), lambda j, i: (i, 0)),
                  pl.BlockSpec((k, bn), lambda j, i: (0, j))],
        out_specs=pl.BlockSpec((bm, bn), lambda j, i: (i, j)),
        out_shape=jax.ShapeDtypeStruct((m, n), out_dtype),
        compiler_params=_params(("parallel", "arbitrary"), 48),
        name=name,
    )(a, b)


def _lru_kernel(u_ref, y_ref, cw_ref, cb_ref, wa_ref, ba_ref, wx_ref, bx_ref, lam_ref,
                o_ref, tail_ref, h_ref, sa_ref, sb_ref, sc_ref, *, tc, nc):
    @pl.when(pl.program_id(2) == 0)
    def _():
        tail_ref[...] = jnp.zeros_like(tail_ref)
        h_ref[...] = jnp.zeros_like(h_ref)

    row8 = lax.broadcasted_iota(I32, (SUBLANES, LANES), 0)
    ones8 = jnp.ones((SUBLANES, LANES), F32)
    zeros8 = jnp.zeros((SUBLANES, LANES), F32)

    def shift_small(x, head8, j):
        r = pltpu.roll(x, j, 0)
        first = jnp.where(row8 < j, pltpu.roll(head8, j, 0), r[:SUBLANES])
        return jnp.concatenate([first, r[SUBLANES:]], axis=0)

    def shift_big(x, fill, d, n):
        return jnp.concatenate([jnp.full((d, LANES), fill, F32), x[:n - d]], axis=0)

    nv = tc // SUBLANES
    sub3 = lax.broadcasted_iota(I32, (nv, SUBLANES, LANES), 1)

    for c in range(nc):
        cs = slice(c * LANES, (c + 1) * LANES)
        u = u_ref[:, cs].astype(F32)
        tail = tail_ref[:, cs]
        cw = cw_ref[:, cs]
        uc = u * cw[CONV_WIDTH - 1:CONV_WIDTH, :] + cb_ref[:, cs]
        for j in range(1, CONV_WIDTH):
            uc = uc + shift_small(u, tail, j) * cw[CONV_WIDTH - 1 - j:CONV_WIDTH - j, :]
        tail_ref[:, cs] = u[tc - SUBLANES:, :]

        ub = uc.astype(BF16)
        r = _sigmoid(jnp.dot(ub, wa_ref[c], preferred_element_type=F32) + ba_ref[:, cs])
        i = _sigmoid(jnp.dot(ub, wx_ref[c], preferred_element_type=F32) + bx_ref[:, cs])
        z = -lam_ref[:, cs]
        softplus = jnp.maximum(z, 0.0) + jnp.log(1.0 + jnp.exp(-jnp.abs(z)))
        a = jnp.exp((-LRU_C * softplus) * r)
        b = jnp.sqrt(1.0 - a * a) * (i * uc)

        a3 = a.reshape(nv, SUBLANES, LANES)
        b3 = b.reshape(nv, SUBLANES, LANES)
        for d in (1, 2, 4):
            keep = sub3 >= d
            a_sh = jnp.where(keep, pltpu.roll(a3, d, 1), 1.0)
            b_sh = jnp.where(keep, pltpu.roll(b3, d, 1), 0.0)
            b3 = a3 * b_sh + b3
            a3 = a3 * a_sh
        sa_ref[c] = a3.reshape(tc, LANES)
        sb_ref[c] = b3.reshape(tc, LANES)
        ta = sa_ref[c, pl.ds(SUBLANES - 1, nv, stride=SUBLANES), :]
        tb = sb_ref[c, pl.ds(SUBLANES - 1, nv, stride=SUBLANES), :]
        d = 1
        while d < nv:
            if d < SUBLANES:
                a_sh = shift_small(ta, ones8, d)
                b_sh = shift_small(tb, zeros8, d)
            else:
                a_sh = shift_big(ta, 1.0, d, nv)
                b_sh = shift_big(tb, 0.0, d, nv)
            tb = ta * b_sh + tb
            ta = ta * a_sh
            d *= 2
        h0 = h_ref[:, cs]
        h_end = tb + ta * h0
        h_ref[:, cs] = h_end[nv - 1:nv, :]
        sc_ref[c] = shift_small(h_end, jnp.broadcast_to(h0, (SUBLANES, LANES)), 1)
        h = jnp.concatenate([b3[j] + a3[j] * sc_ref[c, j:j + 1, :] for j in range(nv)], axis=0)

        y = y_ref[:, cs].astype(F32)
        gelu = 0.5 * y * (1.0 + jnp.tanh(math.sqrt(2.0 / math.pi) * (y + 0.044715 * (y * y * y))))
        o_ref[:, cs] = (h * gelu).astype(o_ref.dtype)


def _lru(proj, conv_w, conv_b, wa, ba, wx, bx, lam, batch, seq, width, col_u, col_y):
    tc, nc = 256, 4
    tc = min(tc, seq)
    cw = nc * LANES
    nt = seq // tc
    t_total = batch * seq
    vec = pl.BlockSpec((1, cw), lambda b, c, t: (0, c))
    gate = pl.BlockSpec((nc, LRU_BLOCK, LRU_BLOCK), lambda b, c, t: (c, 0, 0))
    return pl.pallas_call(
        functools.partial(_lru_kernel, tc=tc, nc=nc),
        grid=(batch, width // cw, nt),
        in_specs=[pl.BlockSpec((tc, cw), lambda b, c, t: (b * nt + t, col_u // cw + c)),
                  pl.BlockSpec((tc, cw), lambda b, c, t: (b * nt + t, col_y // cw + c)),
                  pl.BlockSpec((CONV_WIDTH, cw), lambda b, c, t: (0, c)),
                  vec, gate, vec, gate, vec, vec],
        out_specs=pl.BlockSpec((tc, cw), lambda b, c, t: (b * nt + t, c)),
        out_shape=jax.ShapeDtypeStruct((t_total, width), BF16),
        scratch_shapes=[pltpu.VMEM((SUBLANES, cw), F32), pltpu.VMEM((1, cw), F32),
                        pltpu.VMEM((nc, tc, LANES), F32), pltpu.VMEM((nc, tc, LANES), F32),
                        pltpu.VMEM((nc, tc // SUBLANES, LANES), F32)],
        compiler_params=_params(("parallel", "parallel", "arbitrary"), 32),
        name="lru",
    )(proj, proj, conv_w, conv_b, wa, ba, wx, bx, lam)


def _attn_kernel(q_ref, kc_ref, kp_ref, vc_ref, vp_ref, cc_ref, sc_ref, cp_ref, sp_ref,
                 o_ref, lse_ref):
    n = pl.program_id(2)
    lane = lax.broadcasted_iota(I32, (Q_BLOCK, HEAD_DIM), 1)
    half = ROT_DIM // 2

    def rope(t, cos, sin):
        swapped = jnp.where(lane < half, pltpu.roll(t, HEAD_DIM - half, 1), pltpu.roll(t, half, 1))
        return t * cos + swapped * sin

    cos_c, sin_c = cc_ref[...], sc_ref[...]
    cos_p, sin_p = cp_ref[...], sp_ref[...]
    qi = lax.broadcasted_iota(I32, (Q_BLOCK, 2 * Q_BLOCK), 0)
    ki = lax.broadcasted_iota(I32, (Q_BLOCK, 2 * Q_BLOCK), 1)
    delta = qi + Q_BLOCK - ki
    mask = (delta >= 0) & (delta <= Q_BLOCK) & jnp.logical_not((n == 0) & (ki < Q_BLOCK))
    lse_lane = lax.broadcasted_iota(I32, (Q_BLOCK, LANES), 1)
    lse_all = jnp.zeros((Q_BLOCK, LANES), F32)
    scale = HEAD_DIM ** -0.5
    for h in range(HEADS_PER_GROUP):
        hs = slice(h * HEAD_DIM, (h + 1) * HEAD_DIM)
        q = rope(q_ref[0, :, hs].astype(F32), cos_c, sin_c) * scale
        kc = rope(kc_ref[0, :, hs].astype(F32), cos_c, sin_c)
        kp = rope(kp_ref[0, :, hs].astype(F32), cos_p, sin_p)
        k = jnp.concatenate([kp, kc], axis=0).astype(BF16)
        v = jnp.concatenate([vp_ref[0, :, hs], vc_ref[0, :, hs]], axis=0)
        s = lax.dot_general(q.astype(BF16), k, (((1,), (1,)), ((), ())),
                            preferred_element_type=F32)
        s = jnp.where(mask, s, -jnp.inf)
        m = jnp.max(s, axis=-1, keepdims=True)
        p = jnp.exp(s - m)
        l = jnp.sum(p, axis=-1, keepdims=True)
        o = jnp.dot(p.astype(BF16), v, preferred_element_type=F32) / l
        o_ref[0, :, hs] = o.astype(o_ref.dtype)
        lse_all = jnp.where(lse_lane == h, m + jnp.log(l), lse_all)
    lse_ref[0] = lse_all


def _attention_group(qkv, cos_t, sin_t, batch, seq, dilation):
    d = dilation
    sub_len = seq // d
    nb = sub_len // Q_BLOCK
    cpb = 3
    proj_v = qkv.reshape(batch, sub_len, d * cpb * GROUP_WIDTH)
    cos_v = cos_t.reshape(sub_len, d * HEAD_DIM)
    sin_v = sin_t.reshape(sub_len, d * HEAD_DIM)
    col_q, col_k, col_v = 0, 1, 2

    def cur(col):
        return pl.BlockSpec((1, Q_BLOCK, GROUP_WIDTH), lambda b, r, n: (b, n, r * cpb + col))

    def prev(col):
        return pl.BlockSpec((1, Q_BLOCK, GROUP_WIDTH),
                            lambda b, r, n: (b, jnp.maximum(n - 1, 0), r * cpb + col))

    tab_c = pl.BlockSpec((Q_BLOCK, HEAD_DIM), lambda b, r, n: (n, r))
    tab_p = pl.BlockSpec((Q_BLOCK, HEAD_DIM), lambda b, r, n: (jnp.maximum(n - 1, 0), r))
    o, lse = pl.pallas_call(
        _attn_kernel,
        grid=(batch, d, nb),
        in_specs=[cur(col_q), cur(col_k), prev(col_k), cur(col_v), prev(col_v),
                  tab_c, tab_c, tab_p, tab_p],
        out_specs=[pl.BlockSpec((1, Q_BLOCK, GROUP_WIDTH), lambda b, r, n: (b, n, r)),
                   pl.BlockSpec((1, Q_BLOCK, LANES), lambda b, r, n: (b, n, r))],
        out_shape=[jax.ShapeDtypeStruct((batch, sub_len, d * GROUP_WIDTH), BF16),
                   jax.ShapeDtypeStruct((batch, sub_len, d * LANES), F32)],
        compiler_params=_params(("parallel", "parallel", "arbitrary"), 32),
        name=f"attn_d{d}",
    )(proj_v, proj_v, proj_v, proj_v, proj_v, cos_v, sin_v, cos_v, sin_v)
    return o.reshape(batch * seq, GROUP_WIDTH), lse.reshape(batch * seq, LANES)


def _rope_tables(seq):
    half = ROT_DIM // 2
    inv_freq = jnp.power(ROPE_THETA, -jnp.arange(half, dtype=F32) * (2.0 / ROT_DIM))
    ang = jnp.arange(seq, dtype=jnp.int32).astype(F32)[:, None] * inv_freq[None, :]
    cos, sin = jnp.cos(ang), jnp.sin(ang)
    rest = HEAD_DIM - ROT_DIM
    cos_t = jnp.concatenate([cos, cos, jnp.ones((seq, rest), F32)], axis=1)
    sin_t = jnp.concatenate([-sin, sin, jnp.zeros((seq, rest), F32)], axis=1)
    return cos_t, sin_t


def _mix_kernel(lru_ref, o1_ref, o2_ref, o3_ref, l1_ref, l2_ref, l3_ref, gl_ref, ga_ref, x_ref,
                wl_ref, wat_ref, wo_ref, g_ref, b_ref, x1_ref, x1p_ref):
    l1, l2, l3 = l1_ref[...], l2_ref[...], l3_ref[...]
    m = jnp.maximum(jnp.maximum(l1, l2), l3)
    e1, e2, e3 = jnp.exp(l1 - m), jnp.exp(l2 - m), jnp.exp(l3 - m)
    inv = 1.0 / (e1 + e2 + e3)
    w1, w2, w3 = e1 * inv, e2 * inv, e3 * inv
    parts = []
    for h in range(HEADS_PER_GROUP):
        hs = slice(h * HEAD_DIM, (h + 1) * HEAD_DIM)
        parts.append(w1[:, h:h + 1] * o1_ref[:, hs].astype(F32)
                     + w2[:, h:h + 1] * o2_ref[:, hs].astype(F32)
                     + w3[:, h:h + 1] * o3_ref[:, hs].astype(F32))
    attn = jnp.concatenate(parts, axis=1).astype(BF16)
    lru_p = jnp.dot(lru_ref[...], wl_ref[...], preferred_element_type=F32)
    att_p = jnp.dot(attn, wat_ref[...], preferred_element_type=F32)
    merged = (_sigmoid(gl_ref[...].astype(F32)) * lru_p
              + _sigmoid(ga_ref[...].astype(F32)) * att_p)
    mix = jnp.dot(merged.astype(BF16), wo_ref[...], preferred_element_type=F32)
    x1 = _layer_norm(ALPHA * x_ref[...] + mix, g_ref[...], b_ref[...])
    x1_ref[...] = x1
    _store_token_tiles(x1p_ref, _pack_halves(x1))


def _mix(lru, outs, lses, proj, x, w_lru, w_attn, w_out, g, b):
    t_total, d_model = x.shape
    col_gl, col_ga = 0, d_model
    tm = min(256, t_total)
    row = lambda i: (i, 0)
    const = lambda i: (0, 0)
    resident = functools.partial(pl.BlockSpec, index_map=const, pipeline_mode=pl.Buffered(1))
    return pl.pallas_call(
        _mix_kernel,
        grid=(t_total // tm,),
        in_specs=[pl.BlockSpec((tm, lru.shape[1]), row)]
        + [pl.BlockSpec((tm, GROUP_WIDTH), row)] * 3
        + [pl.BlockSpec((tm, LANES), row)] * 3
        + [pl.BlockSpec((tm, d_model), lambda i: (i, col_gl // d_model)),
           pl.BlockSpec((tm, d_model), lambda i: (i, col_ga // d_model)),
           pl.BlockSpec((tm, d_model), row),
           resident(w_lru.shape), resident(w_attn.shape), resident(w_out.shape),
           pl.BlockSpec((1, d_model), const), pl.BlockSpec((1, d_model), const)],
        out_specs=[pl.BlockSpec((tm, d_model), row), pl.BlockSpec((tm * SUBLANES, LANES), row)],
        out_shape=[jax.ShapeDtypeStruct((t_total, d_model), F32),
                   jax.ShapeDtypeStruct((t_total * SUBLANES, LANES), U32)],
        compiler_params=_params(("parallel",), 56),
        name="mix",
    )(lru, *outs, *lses, proj, proj, x, w_lru, w_attn, w_out, g, b)


def _router_kernel(x_ref, rwt_ref, bias_ref, eidx_ref, rank_ref, gate_ref, cnt_ref, carry_ref,
                   *, tm):
    @pl.when(pl.program_id(0) == 0)
    def _():
        carry_ref[...] = jnp.zeros_like(carry_ref)

    neg_inf = -jnp.inf
    logits = lax.dot_general(rwt_ref[...], x_ref[...], (((1,), (1,)), ((), ())),
                             precision=lax.Precision.HIGHEST, preferred_element_type=F32)
    scores = _sigmoid_exp(logits)
    biased = scores + bias_ref[...]

    sub = lax.broadcasted_iota(I32, (GROUP_SIZE, tm), 0).astype(F32)
    group_score = []
    for g in range(N_EXPERT_GROUPS):
        v = biased[g * GROUP_SIZE:(g + 1) * GROUP_SIZE, :]
        m1 = jnp.max(v, axis=0, keepdims=True)
        first = jnp.min(jnp.where(v == m1, sub, float(GROUP_SIZE)), axis=0, keepdims=True)
        m2 = jnp.max(jnp.where(sub == first, neg_inf, v), axis=0, keepdims=True)
        group_score.append(m1 + m2)

    pieces = []
    for g in range(N_EXPERT_GROUPS):
        beaten_by = jnp.zeros((1, tm), F32)
        for o in range(N_EXPERT_GROUPS):
            if o == g:
                continue
            wins = group_score[o] > group_score[g]
            if o < g:
                wins = wins | (group_score[o] == group_score[g])
            beaten_by = beaten_by + wins.astype(F32)
        keep = jnp.broadcast_to(beaten_by < float(TOPK_GROUPS), (GROUP_SIZE, tm))
        pieces.append(jnp.where(keep, biased[g * GROUP_SIZE:(g + 1) * GROUP_SIZE, :], neg_inf))
    vals = jnp.concatenate(pieces, axis=0)

    row = lax.broadcasted_iota(I32, (N_EXPERTS, tm), 0).astype(F32)
    chosen = []
    selected = jnp.zeros((N_EXPERTS, tm), F32)
    for _ in range(TOP_K):
        m = jnp.max(vals, axis=0, keepdims=True)
        idx = jnp.min(jnp.where(vals == m, row, float(N_EXPERTS)), axis=0, keepdims=True)
        hit = row == idx
        vals = jnp.where(hit, neg_inf, vals)
        selected = jnp.where(hit, 1.0, selected)
        chosen.append(idx)

    before = (lax.broadcasted_iota(I32, (tm, tm), 0) < lax.broadcasted_iota(I32, (tm, tm), 1))
    excl = jnp.dot(selected.astype(BF16), before.astype(BF16), preferred_element_type=F32)
    rank_dense = carry_ref[...] + excl
    carry_ref[...] = carry_ref[...] + jnp.sum(selected, axis=1, keepdims=True)
    cnt_ref[...] = carry_ref[...]

    gates = []
    for k in range(TOP_K):
        hit = row == chosen[k]
        gates.append(jnp.sum(jnp.where(hit, scores, 0.0), axis=0, keepdims=True))
        rank_k = jnp.sum(jnp.where(hit, rank_dense, 0.0), axis=0, keepdims=True)
        eidx_ref[k:k + 1, :] = chosen[k].astype(I32)
        rank_ref[k:k + 1, :] = rank_k.astype(I32)
    total = gates[0]
    for k in range(1, TOP_K):
        total = total + gates[k]
    for k in range(TOP_K):
        gate_ref[k:k + 1, :] = gates[k] / total * ROUTED_SCALE


def _router(x1, router_wt, router_bias):
    t_total, d_model = x1.shape
    tm = min(512, t_total)
    tok = pl.BlockSpec((TOP_K, tm), lambda i: (0, i))
    return pl.pallas_call(
        functools.partial(_router_kernel, tm=tm),
        grid=(t_total // tm,),
        in_specs=[pl.BlockSpec((tm, d_model), lambda i: (i, 0)),
                  pl.BlockSpec((N_EXPERTS, d_model), lambda i: (0, 0)),
                  pl.BlockSpec((N_EXPERTS, 1), lambda i: (0, 0))],
        out_specs=[tok, tok, tok, pl.BlockSpec((N_EXPERTS, 1), lambda i: (0, 0))],
        out_shape=[jax.ShapeDtypeStruct((TOP_K, t_total), I32),
                   jax.ShapeDtypeStruct((TOP_K, t_total), I32),
                   jax.ShapeDtypeStruct((TOP_K, t_total), F32),
                   jax.ShapeDtypeStruct((N_EXPERTS, 1), F32)],
        scratch_shapes=[pltpu.VMEM((N_EXPERTS, 1), F32)],
        compiler_params=_params(("arbitrary",), 32),
        name="router",
    )(x1, router_wt, router_bias)


def _tile_rows(first_row):
    return pl.ds(pl.multiple_of(first_row, SUBLANES), SUBLANES)


def _row_copies(dest_ref, make):
    def start(t, carry):
        for k in range(TOP_K):
            make(t, k, dest_ref[t * TOP_K + k]).start(priority=k % 2)
        return carry

    def wait(t, carry):
        for k in range(TOP_K):
            make(t, k, dest_ref[t * TOP_K + k]).wait()
        return carry

    return start, wait


def _dispatch_kernel(zstart_ref, dest_ref, x_ref, xs_ref, zero_ref, sem, *, tm, bm):
    @pl.when(pl.program_id(0) == 0)
    def _():
        zero_ref[...] = jnp.zeros_like(zero_ref)

        def fill(e, carry):
            rows = pl.ds(pl.multiple_of(zstart_ref[e], SUBLANES), bm * SUBLANES)
            cp = pltpu.make_async_copy(zero_ref, xs_ref.at[rows, :], sem)
            cp.start()
            cp.wait()
            return carry

        lax.fori_loop(0, N_EXPERTS, fill, 0)

    def make(t, k, row):
        return pltpu.make_async_copy(x_ref.at[_tile_rows(t * SUBLANES), :],
                                     xs_ref.at[_tile_rows(row), :], sem)

    start, wait = _row_copies(dest_ref, make)
    lax.fori_loop(0, tm, start, 0)
    lax.fori_loop(0, tm, wait, 0)


def _dispatch(zstart, dest, x1p, n_blocks, bm):
    t_total = x1p.shape[0] // SUBLANES
    tm = min(256, t_total)
    return pl.pallas_call(
        functools.partial(_dispatch_kernel, tm=tm, bm=bm),
        grid_spec=pltpu.PrefetchScalarGridSpec(
            num_scalar_prefetch=1,
            grid=(t_total // tm,),
            in_specs=[pl.BlockSpec((tm * TOP_K,), lambda i, zs: (i,), memory_space=pltpu.SMEM),
                      pl.BlockSpec((tm * SUBLANES, LANES), lambda i, zs: (i, 0))],
            out_specs=pl.BlockSpec(memory_space=pl.ANY),
            scratch_shapes=[pltpu.VMEM((bm * SUBLANES, LANES), U32), pltpu.SemaphoreType.DMA],
        ),
        out_shape=jax.ShapeDtypeStruct(((n_blocks + 1) * bm * SUBLANES, LANES), U32),
        compiler_params=_params(("arbitrary",), 32),
        name="dispatch",
    )(zstart, dest, x1p)


def _expert_kernel(be_ref, nu_ref, new_ref, xs_ref, wg_ref, wu_ref, wd_ref, ys_ref,
                   wgu_s, wd_s, *, bm):
    del be_ref
    i = pl.program_id(0)
    used = i < nu_ref[0]

    hidden = wd_s.shape[0]

    @pl.when(used & (new_ref[i] == 1))
    def _():
        wgu_s[:, :hidden] = wg_ref[0, 0].astype(BF16)
        wgu_s[:, hidden:] = wu_ref[0, 0].astype(BF16)
        wd_s[...] = wd_ref[0, 0].astype(BF16)

    @pl.when(used)
    def _():
        lo, hi = _unpack_halves(_load_token_tiles(xs_ref, bm))
        x = jnp.concatenate([lo.astype(BF16), hi.astype(BF16)], axis=1)
        gu = jnp.dot(x, wgu_s[...], preferred_element_type=F32)
        h = (_silu(gu[:, :hidden]) * gu[:, hidden:]).astype(BF16)
        y = jnp.dot(h, wd_s[...], preferred_element_type=F32)
        _store_token_tiles(ys_ref, _pack_halves(y))

    @pl.when(jnp.logical_not(used))
    def _():
        ys_ref[...] = jnp.zeros_like(ys_ref)


def _experts(block_e, n_used, new_expert, xs, w_gate, w_up, w_down, layer, bm):
    block_rows = bm * SUBLANES
    n_blocks = xs.shape[0] // block_rows - 1
    d_model, hidden = w_gate.shape[2], w_gate.shape[3]
    expert = lambda i, be, nu, ne: (layer, be[i], 0, 0)
    return pl.pallas_call(
        functools.partial(_expert_kernel, bm=bm),
        grid_spec=pltpu.PrefetchScalarGridSpec(
            num_scalar_prefetch=3,
            grid=(n_blocks,),
            in_specs=[
                pl.BlockSpec((block_rows, LANES),
                             lambda i, be, nu, ne: (jnp.minimum(i, nu[0] - 1), 0)),
                pl.BlockSpec((1, 1, d_model, hidden), expert),
                pl.BlockSpec((1, 1, d_model, hidden), expert),
                pl.BlockSpec((1, 1, hidden, d_model), expert),
            ],
            out_specs=pl.BlockSpec((block_rows, LANES),
                                   lambda i, be, nu, ne: (jnp.where(i < nu[0], i, n_blocks), 0)),
            scratch_shapes=[pltpu.VMEM((d_model, 2 * hidden), BF16),
                            pltpu.VMEM((hidden, d_model), BF16)],
        ),
        out_shape=jax.ShapeDtypeStruct(((n_blocks + 1) * block_rows, LANES), U32),
        compiler_params=_params(("arbitrary",), 56),
        name="experts",
    )(block_e, n_used, new_expert, xs, w_gate, w_up, w_down)


def _combine_kernel(dest_ref, next_ref, ys_ref, x1_ref, gate_ref, sg_ref, su_ref, sd_ref, g_ref,
                    b_ref, o_ref, ob_ref, buf_ref, sem, *, tm):
    i = pl.program_id(0)
    slot = i % 2

    def copies(idx_ref, s):
        def make(t, k, row):
            return pltpu.make_async_copy(ys_ref.at[_tile_rows(row), :],
                                         buf_ref.at[s, k, _tile_rows(t * SUBLANES), :], sem.at[s])
        return _row_copies(idx_ref, make)

    start, wait = copies(dest_ref, slot)
    start_next, _ = copies(next_ref, 1 - slot)

    @pl.when(i == 0)
    def _():
        lax.fori_loop(0, tm, start, 0)

    @pl.when(i + 1 < pl.num_programs(0))
    def _():
        lax.fori_loop(0, tm, start_next, 0)

    x1 = x1_ref[...]
    xb = x1.astype(BF16)
    hs = (_silu(jnp.dot(xb, sg_ref[...], preferred_element_type=F32))
          * jnp.dot(xb, su_ref[...], preferred_element_type=F32)).astype(BF16)
    shared = jnp.dot(hs, sd_ref[...], preferred_element_type=F32)

    lax.fori_loop(0, tm, wait, 0)
    gates = gate_ref[...]
    half = SUBLANES * LANES
    acc_lo = jnp.zeros((tm, half), F32)
    acc_hi = jnp.zeros((tm, half), F32)
    for k in range(TOP_K):
        lo, hi = _unpack_halves(_load_token_tiles(buf_ref.at[slot, k], tm))
        gk = gates[:, k:k + 1]
        acc_lo = acc_lo + gk * lo
        acc_hi = acc_hi + gk * hi
    ffn = jnp.concatenate([acc_lo, acc_hi], axis=1) + shared
    x2 = _layer_norm(ALPHA * x1 + ffn, g_ref[...], b_ref[...])
    o_ref[...] = x2
    ob_ref[...] = x2.astype(BF16)


def _combine(dest, ys, x1, gates, sh_gate, sh_up, sh_down, g, b):
    t_total, d_model = x1.shape
    tm = min(256, t_total)
    n_tiles = t_total // tm
    row = lambda i: (i, 0)
    const = lambda i: (0, 0)
    resident = functools.partial(pl.BlockSpec, index_map=const, pipeline_mode=pl.Buffered(1))
    return pl.pallas_call(
        functools.partial(_combine_kernel, tm=tm),
        grid=(t_total // tm,),
        in_specs=[pl.BlockSpec((tm * TOP_K,), lambda i: (i,), memory_space=pltpu.SMEM),
                  pl.BlockSpec((tm * TOP_K,), lambda i: (jnp.minimum(i + 1, n_tiles - 1),),
                               memory_space=pltpu.SMEM),
                  pl.BlockSpec(memory_space=pl.ANY),
                  pl.BlockSpec((tm, d_model), row),
                  pl.BlockSpec((tm, TOP_K), row),
                  resident(sh_gate.shape), resident(sh_up.shape), resident(sh_down.shape),
                  pl.BlockSpec((1, d_model), const), pl.BlockSpec((1, d_model), const)],
        out_specs=[pl.BlockSpec((tm, d_model), row), pl.BlockSpec((tm, d_model), row)],
        out_shape=[jax.ShapeDtypeStruct((t_total, d_model), F32),
                   jax.ShapeDtypeStruct((t_total, d_model), BF16)],
        scratch_shapes=[pltpu.VMEM((2, TOP_K, tm * SUBLANES, LANES), U32),
                        pltpu.SemaphoreType.DMA((2,))],
        compiler_params=_params(("arbitrary",), 56),
        name="combine",
    )(dest, dest, ys, x1, gates, sh_gate, sh_up, sh_down, g, b)


def _mixer_layer(x, xb, p, batch, seq, cos_t, sin_t):
    d_model = x.shape[1]
    lru_w = p["conv_w"].shape[1]
    attn_w = len(DILATED_GROUPS) * GROUP_WIDTH
    col_q, col_k, col_v = 2 * lru_w, 2 * lru_w + attn_w, 2 * lru_w + 2 * attn_w
    col_g = 2 * lru_w + 3 * attn_w
    w_in = p["w_in"]
    proj_uy = _matmul(xb, w_in[:, :col_q].astype(BF16), 1024, 1024, BF16, "in_proj_lru")
    proj_g = _matmul(xb, w_in[:, col_g:].astype(BF16), 1024, 1024, BF16, "in_proj_gates")

    vec = lambda a: a.reshape(1, -1)
    lru = _lru(proj_uy, p["conv_w"], vec(p["conv_b"]), p["lru_wa"].astype(BF16), vec(p["lru_ba"]),
               p["lru_wx"].astype(BF16), vec(p["lru_bx"]), vec(p["lru_lambda"]),
               batch, seq, lru_w, 0, lru_w)
    outs, lses = [], []
    for g, (window, dilation) in enumerate(DILATED_GROUPS):
        assert window // dilation == Q_BLOCK
        heads = lambda col: w_in[:, col + g * GROUP_WIDTH:col + (g + 1) * GROUP_WIDTH]
        w_qkv = jnp.concatenate([heads(col_q), heads(col_k), heads(col_v)], axis=1).astype(BF16)
        qkv = _matmul(xb, w_qkv, 1024, 3 * GROUP_WIDTH, BF16, f"in_proj_qkv{g}")
        o, l = _attention_group(qkv, cos_t, sin_t, batch, seq, dilation)
        outs.append(o)
        lses.append(l)
    return _mix(lru, outs, lses, proj_g, x, p["w_lru_proj"].astype(BF16),
                p["w_attn_proj"].astype(BF16), p["w_out"].astype(BF16),
                vec(p["ln1_g"]), vec(p["ln1_b"]))


def _moe_layer(x1, x1p, p, expert_weights, layer, bm):
    t_total = x1.shape[0]
    eidx, rank, gates, counts = _router(x1, p["router_w"].T, p["router_bias"].reshape(-1, 1))
    counts = counts[:, 0].astype(I32)
    padded = (counts + bm - 1) // bm * bm
    padded_end = jnp.cumsum(padded)
    padded_start = padded_end - padded
    experts = jnp.arange(N_EXPERTS, dtype=I32)
    dest = rank + jnp.sum(jnp.where(eidx[None] == experts[:, None, None],
                                    padded_start[:, None, None], 0), axis=0)
    n_blocks = t_total * TOP_K // bm + N_EXPERTS
    n_used = (padded_end[-1] // bm).astype(I32).reshape(1)
    block_row = jnp.arange(n_blocks, dtype=I32) * bm
    block_e = jnp.minimum(jnp.sum((padded_end[None, :] <= block_row[:, None]).astype(I32), axis=1),
                          N_EXPERTS - 1)
    new_expert = jnp.concatenate([jnp.ones((1,), I32),
                                  (block_e[1:] != block_e[:-1]).astype(I32)])
    dest = (dest * SUBLANES).T.reshape(-1)
    xs = _dispatch((padded_start + counts) * SUBLANES, dest, x1p, n_blocks, bm)
    ys = _experts(block_e, n_used, new_expert, xs, *expert_weights, layer, bm)
    vec = lambda a: a.reshape(1, -1)
    return _combine(dest, ys, x1, gates.T, p["sh_w_gate"].astype(BF16), p["sh_w_up"].astype(BF16),
                    p["sh_w_down"].astype(BF16), vec(p["ln2_g"]), vec(p["ln2_b"]))


_NAMES = ("w_in", "conv_w", "conv_b", "lru_wa", "lru_ba", "lru_wx", "lru_bx", "lru_lambda",
          "w_lru_proj", "w_attn_proj", "w_out", "ln1_g", "ln1_b", "router_w", "router_bias",
          "exp_w_gate", "exp_w_up", "exp_w_down", "sh_w_gate", "sh_w_up", "sh_w_down",
          "ln2_g", "ln2_b")
_EXPERT_NAMES = ("exp_w_gate", "exp_w_up", "exp_w_down")


@jax.jit
def _forward(x, *weights):
    batch, seq, d_model = x.shape
    cos_t, sin_t = _rope_tables(seq)
    bm = min(512, batch * seq)
    xf = x.reshape(batch * seq, d_model)
    xb = xf.astype(BF16)
    full = dict(zip(_NAMES, weights))
    expert_weights = tuple(full[name] for name in _EXPERT_NAMES)
    for layer in range(DEPTH):
        p = {name: w[layer] for name, w in full.items() if name not in _EXPERT_NAMES}
        x1, x1p = _mixer_layer(xf, xb, p, batch, seq, cos_t, sin_t)
        xf, xb = _moe_layer(x1, x1p, p, expert_weights, layer, bm)
    return xf.reshape(batch, seq, d_model)


def kernel(x, w_in, conv_w, conv_b, lru_wa, lru_ba, lru_wx, lru_bx, lru_lambda, w_lru_proj,
           w_attn_proj, w_out, ln1_g, ln1_b, router_w, router_bias, exp_w_gate, exp_w_up,
           exp_w_down, sh_w_gate, sh_w_up, sh_w_down, ln2_g, ln2_b):
    return _forward(x, w_in, conv_w, conv_b, lru_wa, lru_ba, lru_wx, lru_bx, lru_lambda,
                    w_lru_proj, w_attn_proj, w_out, ln1_g, ln1_b, router_w, router_bias,
                    exp_w_gate, exp_w_up, exp_w_down, sh_w_gate, sh_w_up, sh_w_down, ln2_g, ln2_b)
```

```python
import functools
import math

import jax
import jax.numpy as jnp
from jax import lax
from jax.experimental import pallas as pl
from jax.experimental.pallas import tpu as pltpu

F32 = jnp.float32
BF16 = jnp.bfloat16
U32 = jnp.uint32
I32 = jnp.int32

LANES = 128
SUBLANES = 8

DEPTH = 2
LRU_BLOCK = 128
CONV_WIDTH = 4
LRU_C = 8.0
HEAD_DIM = 128
HEADS_PER_GROUP = 4
DILATED_GROUPS = ((128, 1), (512, 4), (2048, 16))
Q_BLOCK = 128
ROPE_THETA = 500000.0
ROT_DIM = HEAD_DIM // 4
N_EXPERTS = 64
TOP_K = 8
N_EXPERT_GROUPS = 8
GROUP_SIZE = N_EXPERTS // N_EXPERT_GROUPS
TOPK_GROUPS = 4
ROUTED_SCALE = 2.5
LN_EPS = 1e-5
ALPHA = (2 * DEPTH) ** 0.25

GROUP_WIDTH = HEADS_PER_GROUP * HEAD_DIM
HI_MASK = 0xFFFF0000


def _params(sem, vmem_mb):
    return pltpu.CompilerParams(dimension_semantics=sem, vmem_limit_bytes=vmem_mb * 2 ** 20)


def _sigmoid(x):
    return 0.5 * jnp.tanh(0.5 * x) + 0.5


def _sigmoid_exp(x):
    return 1.0 / (1.0 + jnp.exp(-x))


def _silu(x):
    return x * _sigmoid(x)


def _layer_norm(y, g, b):
    mu = jnp.mean(y, axis=-1, keepdims=True)
    yc = y - mu
    var = jnp.mean(yc * yc, axis=-1, keepdims=True)
    return yc * lax.rsqrt(var + LN_EPS) * g + b


def _pack_halves(y):
    c = y.shape[1] // 2
    lo = lax.bitcast_convert_type(y[:, :c].astype(BF16).astype(F32), U32)
    hi = lax.bitcast_convert_type(y[:, c:].astype(BF16).astype(F32), U32)
    return (lo >> 16) | (hi & jnp.uint32(HI_MASK))


def _unpack_halves(w):
    lo = lax.bitcast_convert_type(w << 16, F32)
    hi = lax.bitcast_convert_type(w & jnp.uint32(HI_MASK), F32)
    return lo, hi


def _store_token_tiles(ref, words):
    m = words.shape[0]
    for c in range(SUBLANES):
        ref[pl.ds(c, m, stride=SUBLANES), :] = words[:, c * LANES:(c + 1) * LANES]


def _load_token_tiles(ref, m):
    return jnp.concatenate([ref[pl.ds(c, m, stride=SUBLANES), :] for c in range(SUBLANES)], axis=1)


def _matmul_kernel(a_ref, b_ref, o_ref):
    o_ref[...] = jnp.dot(a_ref[...], b_ref[...], preferred_element_type=F32).astype(o_ref.dtype)


def _matmul(a, b, bm, bn, out_dtype, name):
    m, k = a.shape
    n = b.shape[1]
    bm = min(bm, m)
    return pl.pallas_call(
        _matmul_kernel,
        grid=(n // bn, m // bm),
        in_specs=[pl.BlockSpec((bm, k), lambda j, i: (i, 0)),
                  pl.BlockSpec((k, bn), lambda j, i: (0, j))],
        out_specs=pl.BlockSpec((bm, bn), lambda j, i: (i, j)),
        out_shape=jax.ShapeDtypeStruct((m, n), out_dtype),
        compiler_params=_params(("parallel", "arbitrary"), 48),
        name=name,
    )(a, b)


def _lru_kernel(u_ref, y_ref, cw_ref, cb_ref, wa_ref, ba_ref, wx_ref, bx_ref, lam_ref,
                o_ref, tail_ref, h_ref, sa_ref, sb_ref, sc_ref, *, tc, nc):
    @pl.when(pl.program_id(2) == 0)
    def _():
        tail_ref[...] = jnp.zeros_like(tail_ref)
        h_ref[...] = jnp.zeros_like(h_ref)

    row8 = lax.broadcasted_iota(I32, (SUBLANES, LANES), 0)
    ones8 = jnp.ones((SUBLANES, LANES), F32)
    zeros8 = jnp.zeros((SUBLANES, LANES), F32)

    def shift_small(x, head8, j):
        r = pltpu.roll(x, j, 0)
        first = jnp.where(row8 < j, pltpu.roll(head8, j, 0), r[:SUBLANES])
        return jnp.concatenate([first, r[SUBLANES:]], axis=0)

    def shift_big(x, fill, d, n):
        return jnp.concatenate([jnp.full((d, LANES), fill, F32), x[:n - d]], axis=0)

    nv = tc // SUBLANES
    sub3 = lax.broadcasted_iota(I32, (nv, SUBLANES, LANES), 1)

    for c in range(nc):
        cs = slice(c * LANES, (c + 1) * LANES)
        u = u_ref[:, cs].astype(F32)
        tail = tail_ref[:, cs]
        cw = cw_ref[:, cs]
        uc = u * cw[CONV_WIDTH - 1:CONV_WIDTH, :] + cb_ref[:, cs]
        for j in range(1, CONV_WIDTH):
            uc = uc + shift_small(u, tail, j) * cw[CONV_WIDTH - 1 - j:CONV_WIDTH - j, :]
        tail_ref[:, cs] = u[tc - SUBLANES:, :]

        ub = uc.astype(BF16)
        r = _sigmoid(jnp.dot(ub, wa_ref[c], preferred_element_type=F32) + ba_ref[:, cs])
        i = _sigmoid(jnp.dot(ub, wx_ref[c], preferred_element_type=F32) + bx_ref[:, cs])
        z = -lam_ref[:, cs]
        softplus = jnp.maximum(z, 0.0) + jnp.log(1.0 + jnp.exp(-jnp.abs(z)))
        a = jnp.exp((-LRU_C * softplus) * r)
        b = jnp.sqrt(1.0 - a * a) * (i * uc)

        a3 = a.reshape(nv, SUBLANES, LANES)
        b3 = b.reshape(nv, SUBLANES, LANES)
        for d in (1, 2, 4):
            keep = sub3 >= d
            a_sh = jnp.where(keep, pltpu.roll(a3, d, 1), 1.0)
            b_sh = jnp.where(keep, pltpu.roll(b3, d, 1), 0.0)
            b3 = a3 * b_sh + b3
            a3 = a3 * a_sh
        sa_ref[c] = a3.reshape(tc, LANES)
        sb_ref[c] = b3.reshape(tc, LANES)
        ta = sa_ref[c, pl.ds(SUBLANES - 1, nv, stride=SUBLANES), :]
        tb = sb_ref[c, pl.ds(SUBLANES - 1, nv, stride=SUBLANES), :]
        d = 1
        while d < nv:
            if d < SUBLANES:
                a_sh = shift_small(ta, ones8, d)
                b_sh = shift_small(tb, zeros8, d)
            else:
                a_sh = shift_big(ta, 1.0, d, nv)
                b_sh = shift_big(tb, 0.0, d, nv)
            tb = ta * b_sh + tb
            ta = ta * a_sh
            d *= 2
        h0 = h_ref[:, cs]
        h_end = tb + ta * h0
        h_ref[:, cs] = h_end[nv - 1:nv, :]
        sc_ref[c] = shift_small(h_end, jnp.broadcast_to(h0, (SUBLANES, LANES)), 1)
        h = jnp.concatenate([b3[j] + a3[j] * sc_ref[c, j:j + 1, :] for j in range(nv)], axis=0)

        y = y_ref[:, cs].astype(F32)
        gelu = 0.5 * y * (1.0 + jnp.tanh(math.sqrt(2.0 / math.pi) * (y + 0.044715 * (y * y * y))))
        o_ref[:, cs] = (h * gelu).astype(o_ref.dtype)


def _lru(proj, conv_w, conv_b, wa, ba, wx, bx, lam, batch, seq, width, col_u, col_y):
    tc, nc = 256, 4
    tc = min(tc, seq)
    cw = nc * LANES
    nt = seq // tc
    t_total = batch * seq
    vec = pl.BlockSpec((1, cw), lambda b, c, t: (0, c))
    gate = pl.BlockSpec((nc, LRU_BLOCK, LRU_BLOCK), lambda b, c, t: (c, 0, 0))
    return pl.pallas_call(
        functools.partial(_lru_kernel, tc=tc, nc=nc),
        grid=(batch, width // cw, nt),
        in_specs=[pl.BlockSpec((tc, cw), lambda b, c, t: (b * nt + t, col_u // cw + c)),
                  pl.BlockSpec((tc, cw), lambda b, c, t: (b * nt + t, col_y // cw + c)),
                  pl.BlockSpec((CONV_WIDTH, cw), lambda b, c, t: (0, c)),
                  vec, gate, vec, gate, vec, vec],
        out_specs=pl.BlockSpec((tc, cw), lambda b, c, t: (b * nt + t, c)),
        out_shape=jax.ShapeDtypeStruct((t_total, width), BF16),
        scratch_shapes=[pltpu.VMEM((SUBLANES, cw), F32), pltpu.VMEM((1, cw), F32),
                        pltpu.VMEM((nc, tc, LANES), F32), pltpu.VMEM((nc, tc, LANES), F32),
                        pltpu.VMEM((nc, tc // SUBLANES, LANES), F32)],
        compiler_params=_params(("parallel", "parallel", "arbitrary"), 32),
        name="lru",
    )(proj, proj, conv_w, conv_b, wa, ba, wx, bx, lam)


def _attn_kernel(q_ref, kc_ref, kp_ref, vc_ref, vp_ref, cc_ref, sc_ref, cp_ref, sp_ref,
                 o_ref, lse_ref, *, qb):
    n = pl.program_id(2)
    half = ROT_DIM // 2

    def rope(t, cos, sin):
        lane = lax.broadcasted_iota(I32, t.shape, 1)
        swapped = jnp.where(lane < half, pltpu.roll(t, HEAD_DIM - half, 1), pltpu.roll(t, half, 1))
        return t * cos + swapped * sin

    cos_c, sin_c = cc_ref[...], sc_ref[...]
    cos_p, sin_p = cp_ref[...], sp_ref[...]
    qi = lax.broadcasted_iota(I32, (Q_BLOCK, 2 * Q_BLOCK), 0)
    ki = lax.broadcasted_iota(I32, (Q_BLOCK, 2 * Q_BLOCK), 1)
    delta = qi + Q_BLOCK - ki
    band = (delta >= 0) & (delta <= Q_BLOCK)
    band_first = band & jnp.logical_not((n == 0) & (ki < Q_BLOCK))
    lse_lane = lax.broadcasted_iota(I32, (Q_BLOCK, LANES), 1)
    lse_all = [jnp.zeros((Q_BLOCK, LANES), F32) for _ in range(qb)]
    scale = HEAD_DIM ** -0.5
    for h in range(HEADS_PER_GROUP):
        hs = slice(h * HEAD_DIM, (h + 1) * HEAD_DIM)
        q_all = (rope(q_ref[0, :, hs].astype(F32), cos_c, sin_c) * scale).astype(BF16)
        kc = rope(kc_ref[0, :, hs].astype(F32), cos_c, sin_c)
        kp = rope(kp_ref[0, :, hs].astype(F32), cos_p, sin_p)
        k_all = jnp.concatenate([kp, kc], axis=0).astype(BF16)
        v_all = jnp.concatenate([vp_ref[0, :, hs], vc_ref[0, :, hs]], axis=0)
        for j in range(qb):
            rows = slice(j * Q_BLOCK, (j + 1) * Q_BLOCK)
            keys = slice(j * Q_BLOCK, (j + 2) * Q_BLOCK)
            s = lax.dot_general(q_all[rows], k_all[keys], (((1,), (1,)), ((), ())),
                                preferred_element_type=F32)
            s = jnp.where(band_first if j == 0 else band, s, -jnp.inf)
            m = jnp.max(s, axis=-1, keepdims=True)
            p = jnp.exp(s - m)
            l = jnp.sum(p, axis=-1, keepdims=True)
            o = jnp.dot(p.astype(BF16), v_all[keys], preferred_element_type=F32) / l
            o_ref[0, rows, hs] = o.astype(o_ref.dtype)
            lse_all[j] = jnp.where(lse_lane == h, m + jnp.log(l), lse_all[j])
    for j in range(qb):
        lse_ref[0, j * Q_BLOCK:(j + 1) * Q_BLOCK, :] = lse_all[j]


def _attention_group(qkv, cos_t, sin_t, batch, seq, dilation):
    d = dilation
    sub_len = seq // d
    nb = sub_len // Q_BLOCK
    qb = 4 if nb % 4 == 0 else (2 if nb % 2 == 0 else 1)
    rows = qb * Q_BLOCK
    cpb = 3
    proj_v = qkv.reshape(batch, sub_len, d * cpb * GROUP_WIDTH)
    cos_v = cos_t.reshape(sub_len, d * HEAD_DIM)
    sin_v = sin_t.reshape(sub_len, d * HEAD_DIM)
    col_q, col_k, col_v = 0, 1, 2
    before = lambda n: jnp.maximum(qb * n - 1, 0)

    def cur(col):
        return pl.BlockSpec((1, rows, GROUP_WIDTH), lambda b, r, n: (b, n, r * cpb + col))

    def prev(col):
        return pl.BlockSpec((1, Q_BLOCK, GROUP_WIDTH), lambda b, r, n: (b, before(n), r * cpb + col))

    tab_c = pl.BlockSpec((rows, HEAD_DIM), lambda b, r, n: (n, r))
    tab_p = pl.BlockSpec((Q_BLOCK, HEAD_DIM), lambda b, r, n: (before(n), r))
    o, lse = pl.pallas_call(
        functools.partial(_attn_kernel, qb=qb),
        grid=(batch, d, nb // qb),
        in_specs=[cur(col_q), cur(col_k), prev(col_k), cur(col_v), prev(col_v),
                  tab_c, tab_c, tab_p, tab_p],
        out_specs=[pl.BlockSpec((1, rows, GROUP_WIDTH), lambda b, r, n: (b, n, r)),
                   pl.BlockSpec((1, rows, LANES), lambda b, r, n: (b, n, r))],
        out_shape=[jax.ShapeDtypeStruct((batch, sub_len, d * GROUP_WIDTH), BF16),
                   jax.ShapeDtypeStruct((batch, sub_len, d * LANES), F32)],
        compiler_params=_params(("parallel", "parallel", "arbitrary"), 32),
        name=f"attn_d{d}",
    )(proj_v, proj_v, proj_v, proj_v, proj_v, cos_v, sin_v, cos_v, sin_v)
    return o.reshape(batch * seq, GROUP_WIDTH), lse.reshape(batch * seq, LANES)


def _rope_tables(seq):
    half = ROT_DIM // 2
    inv_freq = jnp.power(ROPE_THETA, -jnp.arange(half, dtype=F32) * (2.0 / ROT_DIM))
    ang = jnp.arange(seq, dtype=jnp.int32).astype(F32)[:, None] * inv_freq[None, :]
    cos, sin = jnp.cos(ang), jnp.sin(ang)
    rest = HEAD_DIM - ROT_DIM
    cos_t = jnp.concatenate([cos, cos, jnp.ones((seq, rest), F32)], axis=1)
    sin_t = jnp.concatenate([-sin, sin, jnp.zeros((seq, rest), F32)], axis=1)
    return cos_t, sin_t


def _mix_kernel(lru_ref, o1_ref, o2_ref, o3_ref, l1_ref, l2_ref, l3_ref, gl_ref, ga_ref, x_ref,
                wl_ref, wat_ref, wo_ref, g_ref, b_ref, x1_ref, x1p_ref):
    l1, l2, l3 = l1_ref[...], l2_ref[...], l3_ref[...]
    m = jnp.maximum(jnp.maximum(l1, l2), l3)
    e1, e2, e3 = jnp.exp(l1 - m), jnp.exp(l2 - m), jnp.exp(l3 - m)
    inv = 1.0 / (e1 + e2 + e3)
    w1, w2, w3 = e1 * inv, e2 * inv, e3 * inv
    parts = []
    for h in range(HEADS_PER_GROUP):
        hs = slice(h * HEAD_DIM, (h + 1) * HEAD_DIM)
        parts.append(w1[:, h:h + 1] * o1_ref[:, hs].astype(F32)
                     + w2[:, h:h + 1] * o2_ref[:, hs].astype(F32)
                     + w3[:, h:h + 1] * o3_ref[:, hs].astype(F32))
    attn = jnp.concatenate(parts, axis=1).astype(BF16)
    lru_p = jnp.dot(lru_ref[...], wl_ref[...], preferred_element_type=F32)
    att_p = jnp.dot(attn, wat_ref[...], preferred_element_type=F32)
    merged = (_sigmoid(gl_ref[...].astype(F32)) * lru_p
              + _sigmoid(ga_ref[...].astype(F32)) * att_p)
    mix = jnp.dot(merged.astype(BF16), wo_ref[...], preferred_element_type=F32)
    x1 = _layer_norm(ALPHA * x_ref[...] + mix, g_ref[...], b_ref[...])
    x1_ref[...] = x1
    _store_token_tiles(x1p_ref, _pack_halves(x1))


def _mix(lru, outs, lses, proj, x, w_lru, w_attn, w_out, g, b):
    t_total, d_model = x.shape
    col_gl, col_ga = 0, d_model
    tm = min(256, t_total)
    row = lambda i: (i, 0)
    const = lambda i: (0, 0)
    resident = functools.partial(pl.BlockSpec, index_map=const, pipeline_mode=pl.Buffered(1))
    return pl.pallas_call(
        _mix_kernel,
        grid=(t_total // tm,),
        in_specs=[pl.BlockSpec((tm, lru.shape[1]), row)]
        + [pl.BlockSpec((tm, GROUP_WIDTH), row)] * 3
        + [pl.BlockSpec((tm, LANES), row)] * 3
        + [pl.BlockSpec((tm, d_model), lambda i: (i, col_gl // d_model)),
           pl.BlockSpec((tm, d_model), lambda i: (i, col_ga // d_model)),
           pl.BlockSpec((tm, d_model), row),
           resident(w_lru.shape), resident(w_attn.shape), resident(w_out.shape),
           pl.BlockSpec((1, d_model), const), pl.BlockSpec((1, d_model), const)],
        out_specs=[pl.BlockSpec((tm, d_model), row), pl.BlockSpec((tm * SUBLANES, LANES), row)],
        out_shape=[jax.ShapeDtypeStruct((t_total, d_model), F32),
                   jax.ShapeDtypeStruct((t_total * SUBLANES, LANES), U32)],
        compiler_params=_params(("parallel",), 56),
        name="mix",
    )(lru, *outs, *lses, proj, proj, x, w_lru, w_attn, w_out, g, b)


def _router_kernel(x_ref, rwt_ref, bias_ref, eidx_ref, rank_ref, gate_ref, cnt_ref, carry_ref,
                   *, tm):
    @pl.when(pl.program_id(0) == 0)
    def _():
        carry_ref[...] = jnp.zeros_like(carry_ref)

    neg_inf = -jnp.inf
    logits = lax.dot_general(rwt_ref[...], x_ref[...], (((1,), (1,)), ((), ())),
                             precision=lax.Precision.HIGHEST, preferred_element_type=F32)
    scores = _sigmoid_exp(logits)
    biased = scores + bias_ref[...]

    sub = lax.broadcasted_iota(I32, (GROUP_SIZE, tm), 0).astype(F32)
    group_score = []
    for g in range(N_EXPERT_GROUPS):
        v = biased[g * GROUP_SIZE:(g + 1) * GROUP_SIZE, :]
        m1 = jnp.max(v, axis=0, keepdims=True)
        first = jnp.min(jnp.where(v == m1, sub, float(GROUP_SIZE)), axis=0, keepdims=True)
        m2 = jnp.max(jnp.where(sub == first, neg_inf, v), axis=0, keepdims=True)
        group_score.append(m1 + m2)

    pieces = []
    for g in range(N_EXPERT_GROUPS):
        beaten_by = jnp.zeros((1, tm), F32)
        for o in range(N_EXPERT_GROUPS):
            if o == g:
                continue
            wins = group_score[o] > group_score[g]
            if o < g:
                wins = wins | (group_score[o] == group_score[g])
            beaten_by = beaten_by + wins.astype(F32)
        keep = jnp.broadcast_to(beaten_by < float(TOPK_GROUPS), (GROUP_SIZE, tm))
        pieces.append(jnp.where(keep, biased[g * GROUP_SIZE:(g + 1) * GROUP_SIZE, :], neg_inf))
    vals = jnp.concatenate(pieces, axis=0)

    row = lax.broadcasted_iota(I32, (N_EXPERTS, tm), 0).astype(F32)
    chosen = []
    selected = jnp.zeros((N_EXPERTS, tm), F32)
    for _ in range(TOP_K):
        m = jnp.max(vals, axis=0, keepdims=True)
        idx = jnp.min(jnp.where(vals == m, row, float(N_EXPERTS)), axis=0, keepdims=True)
        hit = row == idx
        vals = jnp.where(hit, neg_inf, vals)
        selected = jnp.where(hit, 1.0, selected)
        chosen.append(idx)

    before = (lax.broadcasted_iota(I32, (tm, tm), 0) < lax.broadcasted_iota(I32, (tm, tm), 1))
    excl = jnp.dot(selected.astype(BF16), before.astype(BF16), preferred_element_type=F32)
    rank_dense = carry_ref[...] + excl
    carry_ref[...] = carry_ref[...] + jnp.sum(selected, axis=1, keepdims=True)
    cnt_ref[...] = carry_ref[...]

    gates = []
    for k in range(TOP_K):
        hit = row == chosen[k]
        gates.append(jnp.sum(jnp.where(hit, scores, 0.0), axis=0, keepdims=True))
        rank_k = jnp.sum(jnp.where(hit, rank_dense, 0.0), axis=0, keepdims=True)
        eidx_ref[k:k + 1, :] = chosen[k].astype(I32)
        rank_ref[k:k + 1, :] = rank_k.astype(I32)
    total = gates[0]
    for k in range(1, TOP_K):
        total = total + gates[k]
    for k in range(TOP_K):
        gate_ref[k:k + 1, :] = gates[k] / total * ROUTED_SCALE


def _router(x1, router_wt, router_bias):
    t_total, d_model = x1.shape
    tm = min(512, t_total)
    tok = pl.BlockSpec((TOP_K, tm), lambda i: (0, i))
    return pl.pallas_call(
        functools.partial(_router_kernel, tm=tm),
        grid=(t_total // tm,),
        in_specs=[pl.BlockSpec((tm, d_model), lambda i: (i, 0)),
                  pl.BlockSpec((N_EXPERTS, d_model), lambda i: (0, 0)),
                  pl.BlockSpec((N_EXPERTS, 1), lambda i: (0, 0))],
        out_specs=[tok, tok, tok, pl.BlockSpec((N_EXPERTS, 1), lambda i: (0, 0))],
        out_shape=[jax.ShapeDtypeStruct((TOP_K, t_total), I32),
                   jax.ShapeDtypeStruct((TOP_K, t_total), I32),
                   jax.ShapeDtypeStruct((TOP_K, t_total), F32),
                   jax.ShapeDtypeStruct((N_EXPERTS, 1), F32)],
        scratch_shapes=[pltpu.VMEM((N_EXPERTS, 1), F32)],
        compiler_params=_params(("arbitrary",), 32),
        name="router",
    )(x1, router_wt, router_bias)


def _tile_rows(first_row):
    return pl.ds(pl.multiple_of(first_row, SUBLANES), SUBLANES)


def _row_copies(dest_ref, make):
    def start(t, carry):
        for k in range(TOP_K):
            make(t, k, dest_ref[t * TOP_K + k]).start(priority=k % 2)
        return carry

    def wait(t, carry):
        for k in range(TOP_K):
            make(t, k, dest_ref[t * TOP_K + k]).wait()
        return carry

    return start, wait


def _dispatch_kernel(zstart_ref, dest_ref, x_ref, xs_ref, zero_ref, sem, *, tm, bm):
    @pl.when(pl.program_id(0) == 0)
    def _():
        zero_ref[...] = jnp.zeros_like(zero_ref)

        def fill(e, carry):
            rows = pl.ds(pl.multiple_of(zstart_ref[e], SUBLANES), bm * SUBLANES)
            cp = pltpu.make_async_copy(zero_ref, xs_ref.at[rows, :], sem)
            cp.start()
            cp.wait()
            return carry

        lax.fori_loop(0, N_EXPERTS, fill, 0)

    def make(t, k, row):
        return pltpu.make_async_copy(x_ref.at[_tile_rows(t * SUBLANES), :],
                                     xs_ref.at[_tile_rows(row), :], sem)

    start, wait = _row_copies(dest_ref, make)
    lax.fori_loop(0, tm, start, 0)
    lax.fori_loop(0, tm, wait, 0)


def _dispatch(zstart, dest, x1p, n_blocks, bm):
    t_total = x1p.shape[0] // SUBLANES
    tm = min(256, t_total)
    return pl.pallas_call(
        functools.partial(_dispatch_kernel, tm=tm, bm=bm),
        grid_spec=pltpu.PrefetchScalarGridSpec(
            num_scalar_prefetch=1,
            grid=(t_total // tm,),
            in_specs=[pl.BlockSpec((tm * TOP_K,), lambda i, zs: (i,), memory_space=pltpu.SMEM),
                      pl.BlockSpec((tm * SUBLANES, LANES), lambda i, zs: (i, 0))],
            out_specs=pl.BlockSpec(memory_space=pl.ANY),
            scratch_shapes=[pltpu.VMEM((bm * SUBLANES, LANES), U32), pltpu.SemaphoreType.DMA],
        ),
        out_shape=jax.ShapeDtypeStruct(((n_blocks + 1) * bm * SUBLANES, LANES), U32),
        compiler_params=_params(("arbitrary",), 32),
        name="dispatch",
    )(zstart, dest, x1p)


def _expert_kernel(be_ref, nu_ref, new_ref, xs_ref, wg_ref, wu_ref, wd_ref, ys_ref,
                   wgu_s, wd_s, *, bm):
    del be_ref
    i = pl.program_id(0)
    used = i < nu_ref[0]

    hidden = wd_s.shape[0]

    @pl.when(used & (new_ref[i] == 1))
    def _():
        wgu_s[:, :hidden] = wg_ref[0, 0].astype(BF16)
        wgu_s[:, hidden:] = wu_ref[0, 0].astype(BF16)
        wd_s[...] = wd_ref[0, 0].astype(BF16)

    @pl.when(used)
    def _():
        lo, hi = _unpack_halves(_load_token_tiles(xs_ref, bm))
        x = jnp.concatenate([lo.astype(BF16), hi.astype(BF16)], axis=1)
        gu = jnp.dot(x, wgu_s[...], preferred_element_type=F32)
        h = (_silu(gu[:, :hidden]) * gu[:, hidden:]).astype(BF16)
        y = jnp.dot(h, wd_s[...], preferred_element_type=F32)
        _store_token_tiles(ys_ref, _pack_halves(y))

    @pl.when(jnp.logical_not(used))
    def _():
        ys_ref[...] = jnp.zeros_like(ys_ref)


def _experts(block_e, n_used, new_expert, xs, w_gate, w_up, w_down, layer, bm):
    block_rows = bm * SUBLANES
    n_blocks = xs.shape[0] // block_rows - 1
    d_model, hidden = w_gate.shape[2], w_gate.shape[3]
    expert = lambda i, be, nu, ne: (layer, be[i], 0, 0)
    return pl.pallas_call(
        functools.partial(_expert_kernel, bm=bm),
        grid_spec=pltpu.PrefetchScalarGridSpec(
            num_scalar_prefetch=3,
            grid=(n_blocks,),
            in_specs=[
                pl.BlockSpec((block_rows, LANES),
                             lambda i, be, nu, ne: (jnp.minimum(i, nu[0] - 1), 0)),
                pl.BlockSpec((1, 1, d_model, hidden), expert),
                pl.BlockSpec((1, 1, d_model, hidden), expert),
                pl.BlockSpec((1, 1, hidden, d_model), expert),
            ],
            out_specs=pl.BlockSpec((block_rows, LANES),
                                   lambda i, be, nu, ne: (jnp.where(i < nu[0], i, n_blocks), 0)),
            scratch_shapes=[pltpu.VMEM((d_model, 2 * hidden), BF16),
                            pltpu.VMEM((hidden, d_model), BF16)],
        ),
        out_shape=jax.ShapeDtypeStruct(((n_blocks + 1) * block_rows, LANES), U32),
        compiler_params=_params(("arbitrary",), 56),
        name="experts",
    )(block_e, n_used, new_expert, xs, w_gate, w_up, w_down)


def _combine_kernel(dest_ref, ys_ref, x1_ref, gate_ref, sg_ref, su_ref, sd_ref, g_ref,
                    b_ref, o_ref, ob_ref, buf_ref, sem, *, tm):
    def make(t, k, row):
        return pltpu.make_async_copy(ys_ref.at[_tile_rows(row), :],
                                     buf_ref.at[k, _tile_rows(t * SUBLANES), :], sem)

    start, wait = _row_copies(dest_ref, make)
    lax.fori_loop(0, tm, start, 0)

    x1 = x1_ref[...]
    xb = x1.astype(BF16)
    hs = (_silu(jnp.dot(xb, sg_ref[...], preferred_element_type=F32))
          * jnp.dot(xb, su_ref[...], preferred_element_type=F32)).astype(BF16)
    shared = jnp.dot(hs, sd_ref[...], preferred_element_type=F32)

    lax.fori_loop(0, tm, wait, 0)
    gates = gate_ref[...]
    half = SUBLANES * LANES
    acc_lo = jnp.zeros((tm, half), F32)
    acc_hi = jnp.zeros((tm, half), F32)
    for k in range(TOP_K):
        lo, hi = _unpack_halves(_load_token_tiles(buf_ref.at[k], tm))
        gk = gates[:, k:k + 1]
        acc_lo = acc_lo + gk * lo
        acc_hi = acc_hi + gk * hi
    ffn = jnp.concatenate([acc_lo, acc_hi], axis=1) + shared
    x2 = _layer_norm(ALPHA * x1 + ffn, g_ref[...], b_ref[...])
    o_ref[...] = x2
    ob_ref[...] = x2.astype(BF16)


def _combine(dest, ys, x1, gates, sh_gate, sh_up, sh_down, g, b):
    t_total, d_model = x1.shape
    tm = min(256, t_total)
    row = lambda i: (i, 0)
    const = lambda i: (0, 0)
    resident = functools.partial(pl.BlockSpec, index_map=const, pipeline_mode=pl.Buffered(1))
    return pl.pallas_call(
        functools.partial(_combine_kernel, tm=tm),
        grid=(t_total // tm,),
        in_specs=[pl.BlockSpec((tm * TOP_K,), lambda i: (i,), memory_space=pltpu.SMEM),
                  pl.BlockSpec(memory_space=pl.ANY),
                  pl.BlockSpec((tm, d_model), row),
                  pl.BlockSpec((tm, TOP_K), row),
                  resident(sh_gate.shape), resident(sh_up.shape), resident(sh_down.shape),
                  pl.BlockSpec((1, d_model), const), pl.BlockSpec((1, d_model), const)],
        out_specs=[pl.BlockSpec((tm, d_model), row), pl.BlockSpec((tm, d_model), row)],
        out_shape=[jax.ShapeDtypeStruct((t_total, d_model), F32),
                   jax.ShapeDtypeStruct((t_total, d_model), BF16)],
        scratch_shapes=[pltpu.VMEM((TOP_K, tm * SUBLANES, LANES), U32), pltpu.SemaphoreType.DMA],
        compiler_params=_params(("arbitrary",), 48),
        name="combine",
    )(dest, ys, x1, gates, sh_gate, sh_up, sh_down, g, b)


def _mixer_layer(x, xb, p, batch, seq, cos_t, sin_t):
    d_model = x.shape[1]
    lru_w = p["conv_w"].shape[1]
    attn_w = len(DILATED_GROUPS) * GROUP_WIDTH
    col_q, col_k, col_v = 2 * lru_w, 2 * lru_w + attn_w, 2 * lru_w + 2 * attn_w
    col_g = 2 * lru_w + 3 * attn_w
    w_in = p["w_in"]
    proj_uy = _matmul(xb, w_in[:, :col_q].astype(BF16), 1024, 1024, BF16, "in_proj_lru")
    proj_g = _matmul(xb, w_in[:, col_g:].astype(BF16), 1024, 1024, BF16, "in_proj_gates")

    vec = lambda a: a.reshape(1, -1)
    lru = _lru(proj_uy, p["conv_w"], vec(p["conv_b"]), p["lru_wa"].astype(BF16), vec(p["lru_ba"]),
               p["lru_wx"].astype(BF16), vec(p["lru_bx"]), vec(p["lru_lambda"]),
               batch, seq, lru_w, 0, lru_w)
    outs, lses = [], []
    for g, (window, dilation) in enumerate(DILATED_GROUPS):
        assert window // dilation == Q_BLOCK
        heads = lambda col: w_in[:, col + g * GROUP_WIDTH:col + (g + 1) * GROUP_WIDTH]
        w_qkv = jnp.concatenate([heads(col_q), heads(col_k), heads(col_v)], axis=1).astype(BF16)
        qkv = _matmul(xb, w_qkv, 1024, 3 * GROUP_WIDTH, BF16, f"in_proj_qkv{g}")
        o, l = _attention_group(qkv, cos_t, sin_t, batch, seq, dilation)
        outs.append(o)
        lses.append(l)
    return _mix(lru, outs, lses, proj_g, x, p["w_lru_proj"].astype(BF16),
                p["w_attn_proj"].astype(BF16), p["w_out"].astype(BF16),
                vec(p["ln1_g"]), vec(p["ln1_b"]))


def _moe_layer(x1, x1p, p, expert_weights, layer, bm):
    t_total = x1.shape[0]
    eidx, rank, gates, counts = _router(x1, p["router_w"].T, p["router_bias"].reshape(-1, 1))
    counts = counts[:, 0].astype(I32)
    padded = (counts + bm - 1) // bm * bm
    padded_end = jnp.cumsum(padded)
    padded_start = padded_end - padded
    experts = jnp.arange(N_EXPERTS, dtype=I32)
    dest = rank + jnp.sum(jnp.where(eidx[None] == experts[:, None, None],
                                    padded_start[:, None, None], 0), axis=0)
    n_blocks = t_total * TOP_K // bm + N_EXPERTS
    n_used = (padded_end[-1] // bm).astype(I32).reshape(1)
    block_row = jnp.arange(n_blocks, dtype=I32) * bm
    block_e = jnp.minimum(jnp.sum((padded_end[None, :] <= block_row[:, None]).astype(I32), axis=1),
                          N_EXPERTS - 1)
    new_expert = jnp.concatenate([jnp.ones((1,), I32),
                                  (block_e[1:] != block_e[:-1]).astype(I32)])
    dest = (dest * SUBLANES).T.reshape(-1)
    xs = _dispatch((padded_start + counts) * SUBLANES, dest, x1p, n_blocks, bm)
    ys = _experts(block_e, n_used, new_expert, xs, *expert_weights, layer, bm)
    vec = lambda a: a.reshape(1, -1)
    return _combine(dest, ys, x1, gates.T, p["sh_w_gate"].astype(BF16), p["sh_w_up"].astype(BF16),
                    p["sh_w_down"].astype(BF16), vec(p["ln2_g"]), vec(p["ln2_b"]))


_NAMES = ("w_in", "conv_w", "conv_b", "lru_wa", "lru_ba", "lru_wx", "lru_bx", "lru_lambda",
          "w_lru_proj", "w_attn_proj", "w_out", "ln1_g", "ln1_b", "router_w", "router_bias",
          "exp_w_gate", "exp_w_up", "exp_w_down", "sh_w_gate", "sh_w_up", "sh_w_down",
          "ln2_g", "ln2_b")
_EXPERT_NAMES = ("exp_w_gate", "exp_w_up", "exp_w_down")


@jax.jit
def _forward(x, *weights):
    batch, seq, d_model = x.shape
    cos_t, sin_t = _rope_tables(seq)
    bm = min(512, batch * seq)
    xf = x.reshape(batch * seq, d_model)
    xb = xf.astype(BF16)
    full = dict(zip(_NAMES, weights))
    expert_weights = tuple(full[name] for name in _EXPERT_NAMES)
    for layer in range(DEPTH):
        p = {name: w[layer] for name, w in full.items() if name not in _EXPERT_NAMES}
        x1, x1p = _mixer_layer(xf, xb, p, batch, seq, cos_t, sin_t)
        xf, xb = _moe_layer(x1, x1p, p, expert_weights, layer, bm)
    return xf.reshape(batch, seq, d_model)


def kernel(x, w_in, conv_w, conv_b, lru_wa, lru_ba, lru_wx, lru_bx, lru_lambda, w_lru_proj,
           w_attn_proj, w_out, ln1_g, ln1_b, router_w, router_bias, exp_w_gate, exp_w_up,
           exp_w_down, sh_w_gate, sh_w_up, sh_w_down, ln2_g, ln2_b):
    return _forward(x, w_in, conv_w, conv_b, lru_wa, lru_ba, lru_wx, lru_bx, lru_lambda,
                    w_lru_proj, w_attn_proj, w_out, ln1_g, ln1_b, router_w, router_bias,
                    exp_w_gate, exp_w_up, exp_w_down, sh_w_gate, sh_w_up, sh_w_down, ln2_g, ln2_b)
```

```python
import functools
import math

import jax
import jax.numpy as jnp
from jax import lax
from jax.experimental import pallas as pl
from jax.experimental.pallas import tpu as pltpu

F32 = jnp.float32
BF16 = jnp.bfloat16
U32 = jnp.uint32
I32 = jnp.int32

LANES = 128
SUBLANES = 8

DEPTH = 2
LRU_BLOCK = 128
CONV_WIDTH = 4
LRU_C = 8.0
HEAD_DIM = 128
HEADS_PER_GROUP = 4
DILATED_GROUPS = ((128, 1), (512, 4), (2048, 16))
Q_BLOCK = 128
ROPE_THETA = 500000.0
ROT_DIM = HEAD_DIM // 4
N_EXPERTS = 64
TOP_K = 8
N_EXPERT_GROUPS = 8
GROUP_SIZE = N_EXPERTS // N_EXPERT_GROUPS
TOPK_GROUPS = 4
ROUTED_SCALE = 2.5
LN_EPS = 1e-5
ALPHA = (2 * DEPTH) ** 0.25

GROUP_WIDTH = HEADS_PER_GROUP * HEAD_DIM
HI_MASK = 0xFFFF0000


def _params(sem, vmem_mb):
    return pltpu.CompilerParams(dimension_semantics=sem, vmem_limit_bytes=vmem_mb * 2 ** 20)


def _sigmoid(x):
    return 0.5 * jnp.tanh(0.5 * x) + 0.5


def _sigmoid_exp(x):
    return 1.0 / (1.0 + jnp.exp(-x))


def _silu(x):
    return x * _sigmoid(x)


def _layer_norm(y, g, b):
    mu = jnp.mean(y, axis=-1, keepdims=True)
    yc = y - mu
    var = jnp.mean(yc * yc, axis=-1, keepdims=True)
    return yc * lax.rsqrt(var + LN_EPS) * g + b


def _pack_halves(y):
    c = y.shape[1] // 2
    lo = lax.bitcast_convert_type(y[:, :c].astype(BF16).astype(F32), U32)
    hi = lax.bitcast_convert_type(y[:, c:].astype(BF16).astype(F32), U32)
    return (lo >> 16) | (hi & jnp.uint32(HI_MASK))


def _unpack_halves(w):
    lo = lax.bitcast_convert_type(w << 16, F32)
    hi = lax.bitcast_convert_type(w & jnp.uint32(HI_MASK), F32)
    return lo, hi


def _store_token_tiles(ref, words):
    m = words.shape[0]
    for c in range(SUBLANES):
        ref[pl.ds(c, m, stride=SUBLANES), :] = words[:, c * LANES:(c + 1) * LANES]


def _load_token_tiles(ref, m):
    return jnp.concatenate([ref[pl.ds(c, m, stride=SUBLANES), :] for c in range(SUBLANES)], axis=1)


def _matmul_kernel(a_ref, b_ref, o_ref):
    o_ref[...] = jnp.dot(a_ref[...], b_ref[...], preferred_element_type=F32).astype(o_ref.dtype)


def _matmul(a, b, bm, bn, out_dtype, name):
    m, k = a.shape
    n = b.shape[1]
    bm = min(bm, m)
    return pl.pallas_call(
        _matmul_kernel,
        grid=(n // bn, m // bm),
        in_specs=[pl.BlockSpec((bm, k), lambda j, i: (i, 0)),
                  pl.BlockSpec((k, bn), lambda j, i: (0, j))],
        out_specs=pl.BlockSpec((bm, bn), lambda j, i: (i, j)),
        out_shape=jax.ShapeDtypeStruct((m, n), out_dtype),
        compiler_params=_params(("parallel", "arbitrary"), 48),
        name=name,
    )(a, b)


def _lru_kernel(u_ref, y_ref, cw_ref, cb_ref, wa_ref, ba_ref, wx_ref, bx_ref, lam_ref,
                o_ref, tail_ref, h_ref, sa_ref, sb_ref, sc_ref, *, tc, nc):
    @pl.when(pl.program_id(2) == 0)
    def _():
        tail_ref[...] = jnp.zeros_like(tail_ref)
        h_ref[...] = jnp.zeros_like(h_ref)

    row8 = lax.broadcasted_iota(I32, (SUBLANES, LANES), 0)
    ones8 = jnp.ones((SUBLANES, LANES), F32)
    zeros8 = jnp.zeros((SUBLANES, LANES), F32)

    def shift_small(x, head8, j):
        r = pltpu.roll(x, j, 0)
        first = jnp.where(row8 < j, pltpu.roll(head8, j, 0), r[:SUBLANES])
        return jnp.concatenate([first, r[SUBLANES:]], axis=0)

    def shift_big(x, fill, d, n):
        return jnp.concatenate([jnp.full((d, LANES), fill, F32), x[:n - d]], axis=0)

    nv = tc // SUBLANES
    sub3 = lax.broadcasted_iota(I32, (nv, SUBLANES, LANES), 1)

    for c in range(nc):
        cs = slice(c * LANES, (c + 1) * LANES)
        u = u_ref[:, cs].astype(F32)
        tail = tail_ref[:, cs]
        cw = cw_ref[:, cs]
        uc = u * cw[CONV_WIDTH - 1:CONV_WIDTH, :] + cb_ref[:, cs]
        for j in range(1, CONV_WIDTH):
            uc = uc + shift_small(u, tail, j) * cw[CONV_WIDTH - 1 - j:CONV_WIDTH - j, :]
        tail_ref[:, cs] = u[tc - SUBLANES:, :]

        ub = uc.astype(BF16)
        r = _sigmoid(jnp.dot(ub, wa_ref[c], preferred_element_type=F32) + ba_ref[:, cs])
        i = _sigmoid(jnp.dot(ub, wx_ref[c], preferred_element_type=F32) + bx_ref[:, cs])
        z = -lam_ref[:, cs]
        softplus = jnp.maximum(z, 0.0) + jnp.log(1.0 + jnp.exp(-jnp.abs(z)))
        a = jnp.exp((-LRU_C * softplus) * r)
        b = jnp.sqrt(1.0 - a * a) * (i * uc)

        a3 = a.reshape(nv, SUBLANES, LANES)
        b3 = b.reshape(nv, SUBLANES, LANES)
        for d in (1, 2, 4):
            keep = sub3 >= d
            a_sh = jnp.where(keep, pltpu.roll(a3, d, 1), 1.0)
            b_sh = jnp.where(keep, pltpu.roll(b3, d, 1), 0.0)
            b3 = a3 * b_sh + b3
            a3 = a3 * a_sh
        sa_ref[c] = a3.reshape(tc, LANES)
        sb_ref[c] = b3.reshape(tc, LANES)
        ta = sa_ref[c, pl.ds(SUBLANES - 1, nv, stride=SUBLANES), :]
        tb = sb_ref[c, pl.ds(SUBLANES - 1, nv, stride=SUBLANES), :]
        d = 1
        while d < nv:
            if d < SUBLANES:
                a_sh = shift_small(ta, ones8, d)
                b_sh = shift_small(tb, zeros8, d)
            else:
                a_sh = shift_big(ta, 1.0, d, nv)
                b_sh = shift_big(tb, 0.0, d, nv)
            tb = ta * b_sh + tb
            ta = ta * a_sh
            d *= 2
        h0 = h_ref[:, cs]
        h_end = tb + ta * h0
        h_ref[:, cs] = h_end[nv - 1:nv, :]
        sc_ref[c] = shift_small(h_end, jnp.broadcast_to(h0, (SUBLANES, LANES)), 1)
        h = jnp.concatenate([b3[j] + a3[j] * sc_ref[c, j:j + 1, :] for j in range(nv)], axis=0)

        y = y_ref[:, cs].astype(F32)
        gelu = 0.5 * y * (1.0 + jnp.tanh(math.sqrt(2.0 / math.pi) * (y + 0.044715 * (y * y * y))))
        o_ref[:, cs] = (h * gelu).astype(o_ref.dtype)


def _lru(proj, conv_w, conv_b, wa, ba, wx, bx, lam, batch, seq, width, col_u, col_y):
    tc, nc = 512, 4
    tc = min(tc, seq)
    cw = nc * LANES
    nt = seq // tc
    t_total = batch * seq
    vec = pl.BlockSpec((1, cw), lambda b, c, t: (0, c))
    gate = pl.BlockSpec((nc, LRU_BLOCK, LRU_BLOCK), lambda b, c, t: (c, 0, 0))
    return pl.pallas_call(
        functools.partial(_lru_kernel, tc=tc, nc=nc),
        grid=(batch, width // cw, nt),
        in_specs=[pl.BlockSpec((tc, cw), lambda b, c, t: (b * nt + t, col_u // cw + c)),
                  pl.BlockSpec((tc, cw), lambda b, c, t: (b * nt + t, col_y // cw + c)),
                  pl.BlockSpec((CONV_WIDTH, cw), lambda b, c, t: (0, c)),
                  vec, gate, vec, gate, vec, vec],
        out_specs=pl.BlockSpec((tc, cw), lambda b, c, t: (b * nt + t, c)),
        out_shape=jax.ShapeDtypeStruct((t_total, width), BF16),
        scratch_shapes=[pltpu.VMEM((SUBLANES, cw), F32), pltpu.VMEM((1, cw), F32),
                        pltpu.VMEM((nc, tc, LANES), F32), pltpu.VMEM((nc, tc, LANES), F32),
                        pltpu.VMEM((nc, tc // SUBLANES, LANES), F32)],
        compiler_params=_params(("parallel", "parallel", "arbitrary"), 32),
        name="lru",
    )(proj, proj, conv_w, conv_b, wa, ba, wx, bx, lam)


def _attn_kernel(q_ref, kc_ref, kp_ref, vc_ref, vp_ref, cc_ref, sc_ref, cp_ref, sp_ref,
                 o_ref, lse_ref, *, qb):
    n = pl.program_id(2)
    half = ROT_DIM // 2

    def rope(t, cos, sin):
        lane = lax.broadcasted_iota(I32, t.shape, 1)
        swapped = jnp.where(lane < half, pltpu.roll(t, HEAD_DIM - half, 1), pltpu.roll(t, half, 1))
        return t * cos + swapped * sin

    cos_c, sin_c = cc_ref[...], sc_ref[...]
    cos_p, sin_p = cp_ref[...], sp_ref[...]
    qi = lax.broadcasted_iota(I32, (Q_BLOCK, 2 * Q_BLOCK), 0)
    ki = lax.broadcasted_iota(I32, (Q_BLOCK, 2 * Q_BLOCK), 1)
    delta = qi + Q_BLOCK - ki
    band = (delta >= 0) & (delta <= Q_BLOCK)
    band_first = band & jnp.logical_not((n == 0) & (ki < Q_BLOCK))
    lse_lane = lax.broadcasted_iota(I32, (Q_BLOCK, LANES), 1)
    lse_all = [jnp.zeros((Q_BLOCK, LANES), F32) for _ in range(qb)]
    scale = HEAD_DIM ** -0.5
    for h in range(HEADS_PER_GROUP):
        hs = slice(h * HEAD_DIM, (h + 1) * HEAD_DIM)
        q_all = (rope(q_ref[0, :, hs].astype(F32), cos_c, sin_c) * scale).astype(BF16)
        kc = rope(kc_ref[0, :, hs].astype(F32), cos_c, sin_c)
        kp = rope(kp_ref[0, :, hs].astype(F32), cos_p, sin_p)
        k_all = jnp.concatenate([kp, kc], axis=0).astype(BF16)
        v_all = jnp.concatenate([vp_ref[0, :, hs], vc_ref[0, :, hs]], axis=0)
        for j in range(qb):
            rows = slice(j * Q_BLOCK, (j + 1) * Q_BLOCK)
            keys = slice(j * Q_BLOCK, (j + 2) * Q_BLOCK)
            s = lax.dot_general(q_all[rows], k_all[keys], (((1,), (1,)), ((), ())),
                                preferred_element_type=F32)
            s = jnp.where(band_first if j == 0 else band, s, -jnp.inf)
            m = jnp.max(s, axis=-1, keepdims=True)
            p = jnp.exp(s - m)
            l = jnp.sum(p, axis=-1, keepdims=True)
            o = jnp.dot(p.astype(BF16), v_all[keys], preferred_element_type=F32) / l
            o_ref[0, rows, hs] = o.astype(o_ref.dtype)
            lse_all[j] = jnp.where(lse_lane == h, m + jnp.log(l), lse_all[j])
    for j in range(qb):
        lse_ref[0, j * Q_BLOCK:(j + 1) * Q_BLOCK, :] = lse_all[j]


def _attention_group(qkv, cos_t, sin_t, batch, seq, dilation):
    d = dilation
    sub_len = seq // d
    nb = sub_len // Q_BLOCK
    qb = 4 if nb % 4 == 0 else (2 if nb % 2 == 0 else 1)
    rows = qb * Q_BLOCK
    cpb = 3
    proj_v = qkv.reshape(batch, sub_len, d * cpb * GROUP_WIDTH)
    cos_v = cos_t.reshape(sub_len, d * HEAD_DIM)
    sin_v = sin_t.reshape(sub_len, d * HEAD_DIM)
    col_q, col_k, col_v = 0, 1, 2
    before = lambda n: jnp.maximum(qb * n - 1, 0)

    def cur(col):
        return pl.BlockSpec((1, rows, GROUP_WIDTH), lambda b, r, n: (b, n, r * cpb + col))

    def prev(col):
        return pl.BlockSpec((1, Q_BLOCK, GROUP_WIDTH), lambda b, r, n: (b, before(n), r * cpb + col))

    tab_c = pl.BlockSpec((rows, HEAD_DIM), lambda b, r, n: (n, r))
    tab_p = pl.BlockSpec((Q_BLOCK, HEAD_DIM), lambda b, r, n: (before(n), r))
    o, lse = pl.pallas_call(
        functools.partial(_attn_kernel, qb=qb),
        grid=(batch, d, nb // qb),
        in_specs=[cur(col_q), cur(col_k), prev(col_k), cur(col_v), prev(col_v),
                  tab_c, tab_c, tab_p, tab_p],
        out_specs=[pl.BlockSpec((1, rows, GROUP_WIDTH), lambda b, r, n: (b, n, r)),
                   pl.BlockSpec((1, rows, LANES), lambda b, r, n: (b, n, r))],
        out_shape=[jax.ShapeDtypeStruct((batch, sub_len, d * GROUP_WIDTH), BF16),
                   jax.ShapeDtypeStruct((batch, sub_len, d * LANES), F32)],
        compiler_params=_params(("parallel", "parallel", "arbitrary"), 32),
        name=f"attn_d{d}",
    )(proj_v, proj_v, proj_v, proj_v, proj_v, cos_v, sin_v, cos_v, sin_v)
    return o.reshape(batch * seq, GROUP_WIDTH), lse.reshape(batch * seq, LANES)


def _rope_tables(seq):
    half = ROT_DIM // 2
    inv_freq = jnp.power(ROPE_THETA, -jnp.arange(half, dtype=F32) * (2.0 / ROT_DIM))
    ang = jnp.arange(seq, dtype=jnp.int32).astype(F32)[:, None] * inv_freq[None, :]
    cos, sin = jnp.cos(ang), jnp.sin(ang)
    rest = HEAD_DIM - ROT_DIM
    cos_t = jnp.concatenate([cos, cos, jnp.ones((seq, rest), F32)], axis=1)
    sin_t = jnp.concatenate([-sin, sin, jnp.zeros((seq, rest), F32)], axis=1)
    return cos_t, sin_t


def _mix_kernel(lru_ref, o1_ref, o2_ref, o3_ref, l1_ref, l2_ref, l3_ref, gl_ref, ga_ref, x_ref,
                wl_ref, wat_ref, wo_ref, g_ref, b_ref, x1_ref, x1p_ref):
    l1, l2, l3 = l1_ref[...], l2_ref[...], l3_ref[...]
    m = jnp.maximum(jnp.maximum(l1, l2), l3)
    e1, e2, e3 = jnp.exp(l1 - m), jnp.exp(l2 - m), jnp.exp(l3 - m)
    inv = 1.0 / (e1 + e2 + e3)
    w1, w2, w3 = e1 * inv, e2 * inv, e3 * inv
    parts = []
    for h in range(HEADS_PER_GROUP):
        hs = slice(h * HEAD_DIM, (h + 1) * HEAD_DIM)
        parts.append(w1[:, h:h + 1] * o1_ref[:, hs].astype(F32)
                     + w2[:, h:h + 1] * o2_ref[:, hs].astype(F32)
                     + w3[:, h:h + 1] * o3_ref[:, hs].astype(F32))
    attn = jnp.concatenate(parts, axis=1).astype(BF16)
    lru_p = jnp.dot(lru_ref[...], wl_ref[...], preferred_element_type=F32)
    att_p = jnp.dot(attn, wat_ref[...], preferred_element_type=F32)
    merged = (_sigmoid(gl_ref[...].astype(F32)) * lru_p
              + _sigmoid(ga_ref[...].astype(F32)) * att_p)
    mix = jnp.dot(merged.astype(BF16), wo_ref[...], preferred_element_type=F32)
    x1 = _layer_norm(ALPHA * x_ref[...] + mix, g_ref[...], b_ref[...])
    x1_ref[...] = x1
    _store_token_tiles(x1p_ref, _pack_halves(x1))


def _mix(lru, outs, lses, proj, x, w_lru, w_attn, w_out, g, b):
    t_total, d_model = x.shape
    col_gl, col_ga = 0, d_model
    tm = min(256, t_total)
    row = lambda i: (i, 0)
    const = lambda i: (0, 0)
    resident = functools.partial(pl.BlockSpec, index_map=const, pipeline_mode=pl.Buffered(1))
    return pl.pallas_call(
        _mix_kernel,
        grid=(t_total // tm,),
        in_specs=[pl.BlockSpec((tm, lru.shape[1]), row)]
        + [pl.BlockSpec((tm, GROUP_WIDTH), row)] * 3
        + [pl.BlockSpec((tm, LANES), row)] * 3
        + [pl.BlockSpec((tm, d_model), lambda i: (i, col_gl // d_model)),
           pl.BlockSpec((tm, d_model), lambda i: (i, col_ga // d_model)),
           pl.BlockSpec((tm, d_model), row),
           resident(w_lru.shape), resident(w_attn.shape), resident(w_out.shape),
           pl.BlockSpec((1, d_model), const), pl.BlockSpec((1, d_model), const)],
        out_specs=[pl.BlockSpec((tm, d_model), row), pl.BlockSpec((tm * SUBLANES, LANES), row)],
        out_shape=[jax.ShapeDtypeStruct((t_total, d_model), F32),
                   jax.ShapeDtypeStruct((t_total * SUBLANES, LANES), U32)],
        compiler_params=_params(("parallel",), 56),
        name="mix",
    )(lru, *outs, *lses, proj, proj, x, w_lru, w_attn, w_out, g, b)


def _router_kernel(x_ref, rwt_ref, bias_ref, eidx_ref, rank_ref, gate_ref, cnt_ref, carry_ref,
                   *, tm):
    @pl.when(pl.program_id(0) == 0)
    def _():
        carry_ref[...] = jnp.zeros_like(carry_ref)

    neg_inf = -jnp.inf
    logits = lax.dot_general(rwt_ref[...], x_ref[...], (((1,), (1,)), ((), ())),
                             precision=lax.Precision.HIGHEST, preferred_element_type=F32)
    scores = _sigmoid_exp(logits)
    biased = scores + bias_ref[...]

    sub = lax.broadcasted_iota(I32, (GROUP_SIZE, tm), 0).astype(F32)
    group_score = []
    for g in range(N_EXPERT_GROUPS):
        v = biased[g * GROUP_SIZE:(g + 1) * GROUP_SIZE, :]
        m1 = jnp.max(v, axis=0, keepdims=True)
        first = jnp.min(jnp.where(v == m1, sub, float(GROUP_SIZE)), axis=0, keepdims=True)
        m2 = jnp.max(jnp.where(sub == first, neg_inf, v), axis=0, keepdims=True)
        group_score.append(m1 + m2)

    pieces = []
    for g in range(N_EXPERT_GROUPS):
        beaten_by = jnp.zeros((1, tm), F32)
        for o in range(N_EXPERT_GROUPS):
            if o == g:
                continue
            wins = group_score[o] > group_score[g]
            if o < g:
                wins = wins | (group_score[o] == group_score[g])
            beaten_by = beaten_by + wins.astype(F32)
        keep = jnp.broadcast_to(beaten_by < float(TOPK_GROUPS), (GROUP_SIZE, tm))
        pieces.append(jnp.where(keep, biased[g * GROUP_SIZE:(g + 1) * GROUP_SIZE, :], neg_inf))
    vals = jnp.concatenate(pieces, axis=0)

    row = lax.broadcasted_iota(I32, (N_EXPERTS, tm), 0).astype(F32)
    chosen = []
    selected = jnp.zeros((N_EXPERTS, tm), F32)
    for _ in range(TOP_K):
        m = jnp.max(vals, axis=0, keepdims=True)
        idx = jnp.min(jnp.where(vals == m, row, float(N_EXPERTS)), axis=0, keepdims=True)
        hit = row == idx
        vals = jnp.where(hit, neg_inf, vals)
        selected = jnp.where(hit, 1.0, selected)
        chosen.append(idx)

    before = (lax.broadcasted_iota(I32, (tm, tm), 0) < lax.broadcasted_iota(I32, (tm, tm), 1))
    excl = jnp.dot(selected.astype(BF16), before.astype(BF16), preferred_element_type=F32)
    rank_dense = carry_ref[...] + excl
    carry_ref[...] = carry_ref[...] + jnp.sum(selected, axis=1, keepdims=True)
    cnt_ref[...] = carry_ref[...]

    gates = []
    for k in range(TOP_K):
        hit = row == chosen[k]
        gates.append(jnp.sum(jnp.where(hit, scores, 0.0), axis=0, keepdims=True))
        rank_k = jnp.sum(jnp.where(hit, rank_dense, 0.0), axis=0, keepdims=True)
        eidx_ref[k:k + 1, :] = chosen[k].astype(I32)
        rank_ref[k:k + 1, :] = rank_k.astype(I32)
    total = gates[0]
    for k in range(1, TOP_K):
        total = total + gates[k]
    for k in range(TOP_K):
        gate_ref[k:k + 1, :] = gates[k] / total * ROUTED_SCALE


def _router(x1, router_wt, router_bias):
    t_total, d_model = x1.shape
    tm = min(512, t_total)
    tok = pl.BlockSpec((TOP_K, tm), lambda i: (0, i))
    return pl.pallas_call(
        functools.partial(_router_kernel, tm=tm),
        grid=(t_total // tm,),
        in_specs=[pl.BlockSpec((tm, d_model), lambda i: (i, 0)),
                  pl.BlockSpec((N_EXPERTS, d_model), lambda i: (0, 0)),
                  pl.BlockSpec((N_EXPERTS, 1), lambda i: (0, 0))],
        out_specs=[tok, tok, tok, pl.BlockSpec((N_EXPERTS, 1), lambda i: (0, 0))],
        out_shape=[jax.ShapeDtypeStruct((TOP_K, t_total), I32),
                   jax.ShapeDtypeStruct((TOP_K, t_total), I32),
                   jax.ShapeDtypeStruct((TOP_K, t_total), F32),
                   jax.ShapeDtypeStruct((N_EXPERTS, 1), F32)],
        scratch_shapes=[pltpu.VMEM((N_EXPERTS, 1), F32)],
        compiler_params=_params(("arbitrary",), 32),
        name="router",
    )(x1, router_wt, router_bias)


def _tile_rows(first_row):
    return pl.ds(pl.multiple_of(first_row, SUBLANES), SUBLANES)


def _row_copies(dest_ref, make):
    def start(t, carry):
        for k in range(TOP_K):
            make(t, k, dest_ref[t * TOP_K + k]).start(priority=k % 2)
        return carry

    def wait(t, carry):
        for k in range(TOP_K):
            make(t, k, dest_ref[t * TOP_K + k]).wait()
        return carry

    return start, wait


def _dispatch_kernel(zstart_ref, dest_ref, x_ref, xs_ref, zero_ref, sem, *, tm, bm):
    @pl.when(pl.program_id(0) == 0)
    def _():
        zero_ref[...] = jnp.zeros_like(zero_ref)

        def fill(e, carry):
            rows = pl.ds(pl.multiple_of(zstart_ref[e], SUBLANES), bm * SUBLANES)
            cp = pltpu.make_async_copy(zero_ref, xs_ref.at[rows, :], sem)
            cp.start()
            cp.wait()
            return carry

        lax.fori_loop(0, N_EXPERTS, fill, 0)

    def make(t, k, row):
        return pltpu.make_async_copy(x_ref.at[_tile_rows(t * SUBLANES), :],
                                     xs_ref.at[_tile_rows(row), :], sem)

    start, wait = _row_copies(dest_ref, make)
    lax.fori_loop(0, tm, start, 0)
    lax.fori_loop(0, tm, wait, 0)


def _dispatch(zstart, dest, x1p, n_blocks, bm):
    t_total = x1p.shape[0] // SUBLANES
    tm = min(256, t_total)
    return pl.pallas_call(
        functools.partial(_dispatch_kernel, tm=tm, bm=bm),
        grid_spec=pltpu.PrefetchScalarGridSpec(
            num_scalar_prefetch=1,
            grid=(t_total // tm,),
            in_specs=[pl.BlockSpec((tm * TOP_K,), lambda i, zs: (i,), memory_space=pltpu.SMEM),
                      pl.BlockSpec((tm * SUBLANES, LANES), lambda i, zs: (i, 0))],
            out_specs=pl.BlockSpec(memory_space=pl.ANY),
            scratch_shapes=[pltpu.VMEM((bm * SUBLANES, LANES), U32), pltpu.SemaphoreType.DMA],
        ),
        out_shape=jax.ShapeDtypeStruct(((n_blocks + 1) * bm * SUBLANES, LANES), U32),
        compiler_params=_params(("arbitrary",), 32),
        name="dispatch",
    )(zstart, dest, x1p)


def _expert_kernel(be_ref, nu_ref, new_ref, xs_ref, wg_ref, wu_ref, wd_ref, ys_ref,
                   wgu_s, wd_s, *, bm):
    del be_ref
    i = pl.program_id(0)
    used = i < nu_ref[0]

    hidden = wd_s.shape[0]

    @pl.when(used & (new_ref[i] == 1))
    def _():
        wgu_s[:, :hidden] = wg_ref[0, 0].astype(BF16)
        wgu_s[:, hidden:] = wu_ref[0, 0].astype(BF16)
        wd_s[...] = wd_ref[0, 0].astype(BF16)

    @pl.when(used)
    def _():
        lo, hi = _unpack_halves(_load_token_tiles(xs_ref, bm))
        x = jnp.concatenate([lo.astype(BF16), hi.astype(BF16)], axis=1)
        gu = jnp.dot(x, wgu_s[...], preferred_element_type=F32)
        h = (_silu(gu[:, :hidden]) * gu[:, hidden:]).astype(BF16)
        y = jnp.dot(h, wd_s[...], preferred_element_type=F32)
        _store_token_tiles(ys_ref, _pack_halves(y))

    @pl.when(jnp.logical_not(used))
    def _():
        ys_ref[...] = jnp.zeros_like(ys_ref)


def _experts(block_e, n_used, new_expert, xs, w_gate, w_up, w_down, layer, bm):
    block_rows = bm * SUBLANES
    n_blocks = xs.shape[0] // block_rows - 1
    d_model, hidden = w_gate.shape[2], w_gate.shape[3]
    expert = lambda i, be, nu, ne: (layer, be[i], 0, 0)
    return pl.pallas_call(
        functools.partial(_expert_kernel, bm=bm),
        grid_spec=pltpu.PrefetchScalarGridSpec(
            num_scalar_prefetch=3,
            grid=(n_blocks,),
            in_specs=[
                pl.BlockSpec((block_rows, LANES),
                             lambda i, be, nu, ne: (jnp.minimum(i, nu[0] - 1), 0)),
                pl.BlockSpec((1, 1, d_model, hidden), expert),
                pl.BlockSpec((1, 1, d_model, hidden), expert),
                pl.BlockSpec((1, 1, hidden, d_model), expert),
            ],
            out_specs=pl.BlockSpec((block_rows, LANES),
                                   lambda i, be, nu, ne: (jnp.where(i < nu[0], i, n_blocks), 0)),
            scratch_shapes=[pltpu.VMEM((d_model, 2 * hidden), BF16),
                            pltpu.VMEM((hidden, d_model), BF16)],
        ),
        out_shape=jax.ShapeDtypeStruct(((n_blocks + 1) * block_rows, LANES), U32),
        compiler_params=_params(("arbitrary",), 56),
        name="experts",
    )(block_e, n_used, new_expert, xs, w_gate, w_up, w_down)


def _combine_kernel(dest_ref, ys_ref, x1_ref, gate_ref, sg_ref, su_ref, sd_ref, g_ref,
                    b_ref, o_ref, ob_ref, buf_ref, sem, *, tm):
    def make(t, k, row):
        return pltpu.make_async_copy(ys_ref.at[_tile_rows(row), :],
                                     buf_ref.at[k, _tile_rows(t * SUBLANES), :], sem)

    start, wait = _row_copies(dest_ref, make)
    lax.fori_loop(0, tm, start, 0)

    x1 = x1_ref[...]
    xb = x1.astype(BF16)
    hs = (_silu(jnp.dot(xb, sg_ref[...], preferred_element_type=F32))
          * jnp.dot(xb, su_ref[...], preferred_element_type=F32)).astype(BF16)
    shared = jnp.dot(hs, sd_ref[...], preferred_element_type=F32)

    lax.fori_loop(0, tm, wait, 0)
    gates = gate_ref[...]
    half = SUBLANES * LANES
    acc_lo = jnp.zeros((tm, half), F32)
    acc_hi = jnp.zeros((tm, half), F32)
    for k in range(TOP_K):
        lo, hi = _unpack_halves(_load_token_tiles(buf_ref.at[k], tm))
        gk = gates[:, k:k + 1]
        acc_lo = acc_lo + gk * lo
        acc_hi = acc_hi + gk * hi
    ffn = jnp.concatenate([acc_lo, acc_hi], axis=1) + shared
    x2 = _layer_norm(ALPHA * x1 + ffn, g_ref[...], b_ref[...])
    o_ref[...] = x2
    ob_ref[...] = x2.astype(BF16)


def _combine(dest, ys, x1, gates, sh_gate, sh_up, sh_down, g, b):
    t_total, d_model = x1.shape
    tm = min(256, t_total)
    row = lambda i: (i, 0)
    const = lambda i: (0, 0)
    resident = functools.partial(pl.BlockSpec, index_map=const, pipeline_mode=pl.Buffered(1))
    return pl.pallas_call(
        functools.partial(_combine_kernel, tm=tm),
        grid=(t_total // tm,),
        in_specs=[pl.BlockSpec((tm * TOP_K,), lambda i: (i,), memory_space=pltpu.SMEM),
                  pl.BlockSpec(memory_space=pl.ANY),
                  pl.BlockSpec((tm, d_model), row),
                  pl.BlockSpec((tm, TOP_K), row),
                  resident(sh_gate.shape), resident(sh_up.shape), resident(sh_down.shape),
                  pl.BlockSpec((1, d_model), const), pl.BlockSpec((1, d_model), const)],
        out_specs=[pl.BlockSpec((tm, d_model), row), pl.BlockSpec((tm, d_model), row)],
        out_shape=[jax.ShapeDtypeStruct((t_total, d_model), F32),
                   jax.ShapeDtypeStruct((t_total, d_model), BF16)],
        scratch_shapes=[pltpu.VMEM((TOP_K, tm * SUBLANES, LANES), U32), pltpu.SemaphoreType.DMA],
        compiler_params=_params(("arbitrary",), 48),
        name="combine",
    )(dest, ys, x1, gates, sh_gate, sh_up, sh_down, g, b)


def _mixer_layer(x, xb, p, batch, seq, cos_t, sin_t):
    d_model = x.shape[1]
    lru_w = p["conv_w"].shape[1]
    attn_w = len(DILATED_GROUPS) * GROUP_WIDTH
    col_q, col_k, col_v = 2 * lru_w, 2 * lru_w + attn_w, 2 * lru_w + 2 * attn_w
    col_g = 2 * lru_w + 3 * attn_w
    w_in = p["w_in"]
    proj_uy = _matmul(xb, w_in[:, :col_q].astype(BF16), 1024, 1024, BF16, "in_proj_lru")
    proj_g = _matmul(xb, w_in[:, col_g:].astype(BF16), 1024, 1024, BF16, "in_proj_gates")

    vec = lambda a: a.reshape(1, -1)
    lru = _lru(proj_uy, p["conv_w"], vec(p["conv_b"]), p["lru_wa"].astype(BF16), vec(p["lru_ba"]),
               p["lru_wx"].astype(BF16), vec(p["lru_bx"]), vec(p["lru_lambda"]),
               batch, seq, lru_w, 0, lru_w)
    outs, lses = [], []
    for g, (window, dilation) in enumerate(DILATED_GROUPS):
        assert window // dilation == Q_BLOCK
        heads = lambda col: w_in[:, col + g * GROUP_WIDTH:col + (g + 1) * GROUP_WIDTH]
        w_qkv = jnp.concatenate([heads(col_q), heads(col_k), heads(col_v)], axis=1).astype(BF16)
        qkv = _matmul(xb, w_qkv, 1024, 3 * GROUP_WIDTH, BF16, f"in_proj_qkv{g}")
        o, l = _attention_group(qkv, cos_t, sin_t, batch, seq, dilation)
        outs.append(o)
        lses.append(l)
    return _mix(lru, outs, lses, proj_g, x, p["w_lru_proj"].astype(BF16),
                p["w_attn_proj"].astype(BF16), p["w_out"].astype(BF16),
                vec(p["ln1_g"]), vec(p["ln1_b"]))


def _moe_layer(x1, x1p, p, expert_weights, layer, bm):
    t_total = x1.shape[0]
    eidx, rank, gates, counts = _router(x1, p["router_w"].T, p["router_bias"].reshape(-1, 1))
    counts = counts[:, 0].astype(I32)
    padded = (counts + bm - 1) // bm * bm
    padded_end = jnp.cumsum(padded)
    padded_start = padded_end - padded
    experts = jnp.arange(N_EXPERTS, dtype=I32)
    dest = rank + jnp.sum(jnp.where(eidx[None] == experts[:, None, None],
                                    padded_start[:, None, None], 0), axis=0)
    n_blocks = t_total * TOP_K // bm + N_EXPERTS
    n_used = (padded_end[-1] // bm).astype(I32).reshape(1)
    block_row = jnp.arange(n_blocks, dtype=I32) * bm
    block_e = jnp.minimum(jnp.sum((padded_end[None, :] <= block_row[:, None]).astype(I32), axis=1),
                          N_EXPERTS - 1)
    new_expert = jnp.concatenate([jnp.ones((1,), I32),
                                  (block_e[1:] != block_e[:-1]).astype(I32)])
    dest = (dest * SUBLANES).T.reshape(-1)
    xs = _dispatch((padded_start + counts) * SUBLANES, dest, x1p, n_blocks, bm)
    ys = _experts(block_e, n_used, new_expert, xs, *expert_weights, layer, bm)
    vec = lambda a: a.reshape(1, -1)
    return _combine(dest, ys, x1, gates.T, p["sh_w_gate"].astype(BF16), p["sh_w_up"].astype(BF16),
                    p["sh_w_down"].astype(BF16), vec(p["ln2_g"]), vec(p["ln2_b"]))


_NAMES = ("w_in", "conv_w", "conv_b", "lru_wa", "lru_ba", "lru_wx", "lru_bx", "lru_lambda",
          "w_lru_proj", "w_attn_proj", "w_out", "ln1_g", "ln1_b", "router_w", "router_bias",
          "exp_w_gate", "exp_w_up", "exp_w_down", "sh_w_gate", "sh_w_up", "sh_w_down",
          "ln2_g", "ln2_b")
_EXPERT_NAMES = ("exp_w_gate", "exp_w_up", "exp_w_down")


@jax.jit
def _forward(x, *weights):
    batch, seq, d_model = x.shape
    cos_t, sin_t = _rope_tables(seq)
    bm = min(512, batch * seq)
    xf = x.reshape(batch * seq, d_model)
    xb = xf.astype(BF16)
    full = dict(zip(_NAMES, weights))
    expert_weights = tuple(full[name] for name in _EXPERT_NAMES)
    for layer in range(DEPTH):
        p = {name: w[layer] for name, w in full.items() if name not in _EXPERT_NAMES}
        x1, x1p = _mixer_layer(xf, xb, p, batch, seq, cos_t, sin_t)
        xf, xb = _moe_layer(x1, x1p, p, expert_weights, layer, bm)
    return xf.reshape(batch, seq, d_model)


def kernel(x, w_in, conv_w, conv_b, lru_wa, lru_ba, lru_wx, lru_bx, lru_lambda, w_lru_proj,
           w_attn_proj, w_out, ln1_g, ln1_b, router_w, router_bias, exp_w_gate, exp_w_up,
           exp_w_down, sh_w_gate, sh_w_up, sh_w_down, ln2_g, ln2_b):
    return _forward(x, w_in, conv_w, conv_b, lru_wa, lru_ba, lru_wx, lru_bx, lru_lambda,
                    w_lru_proj, w_attn_proj, w_out, ln1_g, ln1_b, router_w, router_bias,
                    exp_w_gate, exp_w_up, exp_w_down, sh_w_gate, sh_w_up, sh_w_down, ln2_g, ln2_b)
```
